```python
import math
import jax, jax.numpy as jnp
from jax import lax
import numpy as np

D_MODEL = 1024
BATCH = 16
SEQ = 2048
DEPTH = 1
DEC_BATCH = 16
DEC_SEQ = 4096
PAST_LEN = 128

GRID_W = 64
D_MIX = D_MODEL
CONV_WIDTH = D_MIX // 2
N_HEADS = 8
HEAD_DIM = 64
ATTN_WIDTH = N_HEADS * HEAD_DIM
D_PROJ = 3 * CONV_WIDTH + 3 * ATTN_WIDTH
NA_ROWS = 8
NA_COLS = 16
CONV_K = 3
D_FF = 2816
LN_EPS = 1e-5
ALPHA = (2.0 * DEPTH) ** 0.25
BETA = (8.0 * DEPTH) ** -0.25
NEG_INF = -1e30

kernel_name = "hybrid_shortconv_natten_encoder"


def _layernorm(x, g, b):
    xf = x.astype(jnp.float32)
    mu = jnp.mean(xf, axis=-1, keepdims=True)
    var = jnp.mean(jnp.square(xf - mu), axis=-1, keepdims=True)
    return ((xf - mu) * lax.rsqrt(var + LN_EPS) * g.astype(jnp.float32) + b.astype(jnp.float32)).astype(x.dtype)


def _rmsnorm(x, g):
    xf = x.astype(jnp.float32)
    ms = jnp.mean(jnp.square(xf), axis=-1, keepdims=True)
    return (xf * lax.rsqrt(ms + LN_EPS) * g.astype(jnp.float32)).astype(x.dtype)


def _dwconv3(x, w, b):
    c = x.shape[-1]
    y = lax.conv_general_dilated(
        x, w[:, None, :].astype(x.dtype), window_strides=(1,),
        padding=((CONV_K // 2, CONV_K // 2),),
        dimension_numbers=("NWC", "WIO", "NWC"), feature_group_count=c)
    return y + b.astype(x.dtype)


def _neighbourhood_attention(q, k, v, rpb):
    bsz, t = q.shape[0], q.shape[1]
    rows = t // GRID_W
    kh = min(NA_ROWS, rows)
    scale = HEAD_DIM ** -0.5
    qg = (q * scale).reshape(bsz, rows, GRID_W, N_HEADS, HEAD_DIM)
    kg = k.reshape(bsz, rows, GRID_W, N_HEADS, HEAD_DIM)
    vg = v.reshape(bsz, rows, GRID_W, N_HEADS, HEAD_DIM)

    cols = jnp.arange(GRID_W)
    cstart = jnp.clip(cols - NA_COLS // 2, 0, GRID_W - NA_COLS)
    col_ok = (cols[None, :] >= cstart[:, None]) & (cols[None, :] < cstart[:, None] + NA_COLS)
    dc_idx = jnp.clip(cols[None, :] - cols[:, None], -(NA_COLS - 1), NA_COLS - 1) + NA_COLS - 1
    rpb_f = rpb.astype(jnp.float32)
    bias_full = rpb_f[:, :, dc_idx]
    bias_full = jnp.where(col_ok[None, None], bias_full, NEG_INF)
    bias_full = bias_full.transpose(0, 2, 1, 3)

    def row_block(r):
        rs = jnp.clip(r - kh // 2, 0, rows - kh)
        q_r = lax.dynamic_index_in_dim(qg, r, axis=1, keepdims=False)
        k_r = lax.dynamic_slice_in_dim(kg, rs, kh, axis=1)
        v_r = lax.dynamic_slice_in_dim(vg, rs, kh, axis=1)
        bias = lax.dynamic_slice_in_dim(bias_full, rs - r + NA_ROWS - 1, kh, axis=2)
        s = jnp.einsum("bqhd,brkhd->bhqrk", q_r, k_r).astype(jnp.float32) + bias[None]
        p = jax.nn.softmax(s.reshape(bsz, N_HEADS, GRID_W, kh * GRID_W), axis=-1)
        p = p.reshape(s.shape).astype(v.dtype)
        return jnp.einsum("bhqrk,brkhd->bqhd", p, v_r)

    out = lax.map(row_block, jnp.arange(rows))
    return out.transpose(1, 0, 2, 3, 4).reshape(bsz, t, N_HEADS * HEAD_DIM)


def _layer(x, w_in, conv_w, conv_b, rpb, gn_g, w_o, ln1_g, ln1_b,
           w_up, ffn_conv_w, ffn_conv_b, w_down, ln2_g, ln2_b):
    bsz, t, _ = x.shape
    proj = x @ w_in
    c1, c2, c3 = CONV_WIDTH, 2 * CONV_WIDTH, 3 * CONV_WIDTH
    a1, a2 = c3 + ATTN_WIDTH, c3 + 2 * ATTN_WIDTH
    bg, cg, hc, q, k, v = jnp.split(proj, [c1, c2, c3, a1, a2], axis=-1)
    y_conv = bg * _dwconv3(cg * hc, conv_w, conv_b)
    hs = (bsz, t, N_HEADS, HEAD_DIM)
    y_attn = _neighbourhood_attention(q.reshape(hs), k.reshape(hs), v.reshape(hs), rpb)
    y = jnp.concatenate([_rmsnorm(y_conv, gn_g[:CONV_WIDTH]),
                         _rmsnorm(y_attn, gn_g[CONV_WIDTH:])], axis=-1)
    x = _layernorm(ALPHA * x + y @ w_o, ln1_g, ln1_b)
    u = _dwconv3(x @ w_up, ffn_conv_w, ffn_conv_b)
    gate, val = jnp.split(u, 2, axis=-1)
    x = _layernorm(ALPHA * x + (jax.nn.gelu(gate) * val) @ w_down, ln2_g, ln2_b)
    return x


def _trunk(x, ln_in_g, ln_in_b, w_in, conv_w, conv_b, rpb, gn_g, w_o, ln1_g, ln1_b,
           w_up, ffn_conv_w, ffn_conv_b, w_down, ln2_g, ln2_b):
    x = _layernorm(x, ln_in_g, ln_in_b)
    for i in range(DEPTH):
        x = _layer(x, w_in[i], conv_w[i], conv_b[i], rpb[i], gn_g[i], w_o[i], ln1_g[i], ln1_b[i],
                   w_up[i], ffn_conv_w[i], ffn_conv_b[i], w_down[i], ln2_g[i], ln2_b[i])
    return x


def setup_inputs(seed: int = 0) -> dict:
    key = jax.random.key(seed)
    ks = jax.random.split(key, 20)
    f32 = jnp.float32
    nrm = lambda k, shape, s: jax.random.normal(k, shape, f32) * s
    return {
        "x_prompt": nrm(ks[0], (BATCH, SEQ, D_MODEL), 1.0),
        "x_sample": nrm(ks[1], (DEC_BATCH, DEC_SEQ, D_MODEL), 1.0),
        "ln_in_g": 1.0 + nrm(ks[2], (D_MODEL,), 0.02),
        "ln_in_b": nrm(ks[3], (D_MODEL,), 0.02),
        "w_in": nrm(ks[4], (DEPTH, D_MODEL, D_PROJ), D_MODEL ** -0.5),
        "conv_w": nrm(ks[5], (DEPTH, CONV_K, CONV_WIDTH), CONV_K ** -0.5),
        "conv_b": nrm(ks[6], (DEPTH, CONV_WIDTH), 0.02),
        "rpb": nrm(ks[7], (DEPTH, N_HEADS, 2 * NA_ROWS - 1, 2 * NA_COLS - 1), 0.1),
        "gn_g": 1.0 + nrm(ks[8], (DEPTH, D_MIX), 0.02),
        "w_o": nrm(ks[9], (DEPTH, D_MIX, D_MODEL), BETA * D_MIX ** -0.5),
        "ln1_g": 1.0 + nrm(ks[10], (DEPTH, D_MODEL), 0.02),
        "ln1_b": nrm(ks[11], (DEPTH, D_MODEL), 0.02),
        "w_up": nrm(ks[12], (DEPTH, D_MODEL, 2 * D_FF), D_MODEL ** -0.5),
        "ffn_conv_w": nrm(ks[13], (DEPTH, CONV_K, 2 * D_FF), CONV_K ** -0.5),
        "ffn_conv_b": nrm(ks[14], (DEPTH, 2 * D_FF), 0.02),
        "w_down": nrm(ks[15], (DEPTH, D_FF, D_MODEL), BETA * D_FF ** -0.5),
        "ln2_g": 1.0 + nrm(ks[16], (DEPTH, D_MODEL), 0.02),
        "ln2_b": nrm(ks[17], (DEPTH, D_MODEL), 0.02),
    }


def reference(x_prompt, x_sample, ln_in_g, ln_in_b, w_in, conv_w, conv_b, rpb, gn_g, w_o,
              ln1_g, ln1_b, w_up, ffn_conv_w, ffn_conv_b, w_down, ln2_g, ln2_b):
    y_prompt = _trunk(x_prompt, ln_in_g, ln_in_b, w_in, conv_w, conv_b, rpb, gn_g, w_o,
                      ln1_g, ln1_b, w_up, ffn_conv_w, ffn_conv_b, w_down, ln2_g, ln2_b)
    y_sample = _trunk(x_sample, ln_in_g, ln_in_b, w_in, conv_w, conv_b, rpb, gn_g, w_o,
                      ln1_g, ln1_b, w_up, ffn_conv_w, ffn_conv_b, w_down, ln2_g, ln2_b)
    return (y_prompt, y_sample)
```

```python
import functools

import jax
import jax.numpy as jnp
from jax import lax
from jax.experimental import pallas as pl
from jax.experimental.pallas import tpu as pltpu

D_MODEL = 1024
CONV_WIDTH = 512
ATTN_WIDTH = 512
N_HEADS = 8
HEAD_DIM = 64
GRID_W = 64
NA_ROWS = 8
NA_COLS = 16
D_FF = 2816
LN_EPS = 1e-5
ALPHA = 2.0 ** 0.25
NEG_INF = -1e30

QBLK = 2 * GRID_W
KV_PAIRS = 5
KBLK = KV_PAIRS * QBLK
HALO = 16
FF_CHUNK = 256
N_FF_CHUNKS = D_FF // FF_CHUNK
VMEM_LIMIT_BYTES = 56 * 1024 * 1024

_F32 = jnp.float32
_BF16 = jnp.bfloat16


def _layernorm(x, g, b):
    mu = jnp.mean(x, axis=-1, keepdims=True)
    xc = x - mu
    var = jnp.mean(xc * xc, axis=-1, keepdims=True)
    return xc * lax.rsqrt(var + LN_EPS) * g + b


def _rmsnorm(x, g):
    ms = jnp.mean(x * x, axis=-1, keepdims=True)
    return x * lax.rsqrt(ms + LN_EPS) * g


def _dot(a, b):
    return jnp.dot(a, b, preferred_element_type=_F32)


def _inproj_kernel(x_ref, g_ref, b_ref, w_ref, wkt_ref,
                   bg_ref, gg_ref, q_ref, v_ref, kt_ref):
    tm = x_ref.shape[0]
    xn = _layernorm(x_ref[...], g_ref[...], b_ref[...])
    xb = xn.astype(_BF16)
    c1, c2, c3 = CONV_WIDTH, 2 * CONV_WIDTH, 3 * CONV_WIDTH
    pa = _dot(xb, w_ref[:, 0:c3])
    bg_ref[...] = pa[:, 0:c1].astype(_BF16)
    gg_ref[...] = (pa[:, c1:c2] * pa[:, c2:c3]).astype(_BF16)
    q = _dot(xb, w_ref[:, c3:c3 + ATTN_WIDTH])
    q_ref[...] = (q * (HEAD_DIM ** -0.5)).astype(_BF16)
    v_ref[...] = _dot(xb, w_ref[:, c3 + ATTN_WIDTH:c3 + 2 * ATTN_WIDTH]).astype(_BF16)
    kt = lax.dot_general(wkt_ref[...], xb, (((1,), (1,)), ((), ())),
                         preferred_element_type=_F32)
    for i in range(tm // QBLK):
        kt_ref[i] = kt[:, i * QBLK:(i + 1) * QBLK].astype(_BF16)


def _inproj(x2d, ln_g, ln_b, w_main, w_kt, tm):
    n = x2d.shape[0]
    const = lambda i: (0, 0)
    tok = lambda i: (i, 0)
    outs = pl.pallas_call(
        _inproj_kernel,
        grid=(n // tm,),
        in_specs=[
            pl.BlockSpec((tm, D_MODEL), tok),
            pl.BlockSpec((1, D_MODEL), const),
            pl.BlockSpec((1, D_MODEL), const),
            pl.BlockSpec(w_main.shape, const),
            pl.BlockSpec(w_kt.shape, const),
        ],
        out_specs=[
            pl.BlockSpec((tm, CONV_WIDTH), tok),
            pl.BlockSpec((tm, CONV_WIDTH), tok),
            pl.BlockSpec((tm, ATTN_WIDTH), tok),
            pl.BlockSpec((tm, ATTN_WIDTH), tok),
            pl.BlockSpec((tm // QBLK, ATTN_WIDTH, QBLK), lambda i: (i, 0, 0)),
        ],
        out_shape=[
            jax.ShapeDtypeStruct((n, CONV_WIDTH), _BF16),
            jax.ShapeDtypeStruct((n, CONV_WIDTH), _BF16),
            jax.ShapeDtypeStruct((n, ATTN_WIDTH), _BF16),
            jax.ShapeDtypeStruct((n, ATTN_WIDTH), _BF16),
            jax.ShapeDtypeStruct((n // QBLK, ATTN_WIDTH, QBLK), _BF16),
        ],
        compiler_params=pltpu.CompilerParams(
            dimension_semantics=("arbitrary",), vmem_limit_bytes=VMEM_LIMIT_BYTES),
        name="inproj",
    )(x2d, ln_g, ln_b, w_main, w_kt)
    return outs


def _mixer_kernel(nblk, x_ref, bg_ref, gg_ref, gp_ref, gn_ref, q_ref, kt_ref, v_ref,
                  bias_ref, wo_ref, lng_ref, lnb_ref, cw_ref, cb_ref, gng_ref,
                  l1g_ref, l1b_ref, o_ref):
    i = pl.program_id(1)
    p0 = jnp.clip(i - 2, 0, nblk - KV_PAIRS)

    lane = lax.broadcasted_iota(jnp.int32, (QBLK, 2 * HEAD_DIM), 1)
    low = lane < HEAD_DIM
    q = q_ref[0]
    row0 = pl.multiple_of(p0 * QBLK, QBLK)
    ys = []
    for j in range(N_HEADS // 2):
        cs = slice(j * 2 * HEAD_DIM, (j + 1) * 2 * HEAD_DIM)
        qp = q[:, cs]
        ktp = jnp.concatenate([kt_ref[0, p0 + t, cs, :] for t in range(KV_PAIRS)], axis=1)
        vp = v_ref[0, pl.ds(row0, KBLK), cs]
        outs = []
        for a in range(2):
            sel = low if a == 0 else jnp.logical_not(low)
            qa = jnp.where(sel, qp, jnp.zeros_like(qp))
            s = _dot(qa, ktp) + bias_ref[0, 2 * j + a]
            m = jnp.max(s, axis=-1, keepdims=True)
            p = jnp.exp(s - m)
            l = jnp.sum(p, axis=-1, keepdims=True)
            outs.append(_dot(p.astype(_BF16), vp) / l)
        ys.append(jnp.where(low, outs[0], outs[1]))
    y_attn = jnp.concatenate(ys, axis=1)

    g = gg_ref[0].astype(_F32)
    prev_row = gp_ref[0].astype(_F32)[HALO - 1:HALO, :]
    next_row = gn_ref[0].astype(_F32)[0:1, :]
    prev_row = jnp.where(i > 0, prev_row, jnp.zeros_like(prev_row))
    next_row = jnp.where(i < nblk - 1, next_row, jnp.zeros_like(next_row))
    row = lax.broadcasted_iota(jnp.int32, g.shape, 0)
    g_m1 = jnp.where(row == 0, prev_row, pltpu.roll(g, 1, 0))
    g_p1 = jnp.where(row == QBLK - 1, next_row, pltpu.roll(g, QBLK - 1, 0))
    cw = cw_ref[...]
    conv = cw[0:1] * g_m1 + cw[1:2] * g + cw[2:3] * g_p1 + cb_ref[...]
    y_conv = bg_ref[0].astype(_F32) * conv

    gng = gng_ref[...]
    y = jnp.concatenate([_rmsnorm(y_conv, gng[:, :CONV_WIDTH]),
                         _rmsnorm(y_attn, gng[:, CONV_WIDTH:])], axis=1).astype(_BF16)
    xn = _layernorm(x_ref[0], lng_ref[...], lnb_ref[...])
    z = ALPHA * xn + _dot(y, wo_ref[...])
    o_ref[0] = _layernorm(z, l1g_ref[...], l1b_ref[...])


def _mixer(x3d, bg, gg, q, kt, v, bias, w_o, ln_g, ln_b, cw, cb, gng, l1g, l1b):
    bsz, t, _ = x3d.shape
    nblk = t // QBLK
    nhalo = t // HALO
    per = QBLK // HALO

    def bias_type(i):
        return jnp.where(i == 0, 0, jnp.where(i == 1, 1, jnp.where(
            i == nblk - 2, 3, jnp.where(i == nblk - 1, 4, 2))))

    tok = lambda b, i: (b, i, 0)
    c2 = lambda b, i: (0, 0)
    return pl.pallas_call(
        functools.partial(_mixer_kernel, nblk),
        grid=(bsz, nblk),
        in_specs=[
            pl.BlockSpec((1, QBLK, D_MODEL), tok),
            pl.BlockSpec((1, QBLK, CONV_WIDTH), tok),
            pl.BlockSpec((1, QBLK, CONV_WIDTH), tok),
            pl.BlockSpec((1, HALO, CONV_WIDTH), lambda b, i: (b, jnp.maximum(i * per - 1, 0), 0)),
            pl.BlockSpec((1, HALO, CONV_WIDTH),
                         lambda b, i: (b, jnp.minimum((i + 1) * per, nhalo - 1), 0)),
            pl.BlockSpec((1, QBLK, ATTN_WIDTH), tok),
            pl.BlockSpec((1, nblk, ATTN_WIDTH, QBLK), lambda b, i: (b, 0, 0, 0)),
            pl.BlockSpec((1, t, ATTN_WIDTH), lambda b, i: (b, 0, 0)),
            pl.BlockSpec((1, N_HEADS, QBLK, KBLK), lambda b, i: (bias_type(i), 0, 0, 0)),
            pl.BlockSpec((D_MODEL, D_MODEL), c2),
            pl.BlockSpec((1, D_MODEL), c2),
            pl.BlockSpec((1, D_MODEL), c2),
            pl.BlockSpec((3, CONV_WIDTH), c2),
            pl.BlockSpec((1, CONV_WIDTH), c2),
            pl.BlockSpec((1, D_MODEL), c2),
            pl.BlockSpec((1, D_MODEL), c2),
            pl.BlockSpec((1, D_MODEL), c2),
        ],
        out_specs=pl.BlockSpec((1, QBLK, D_MODEL), tok),
        out_shape=jax.ShapeDtypeStruct((bsz, t, D_MODEL), _F32),
        compiler_params=pltpu.CompilerParams(
            dimension_semantics=("arbitrary", "arbitrary"), vmem_limit_bytes=VMEM_LIMIT_BYTES),
        name="mixer",
    )(x3d, bg, gg, gg, gg, q, kt, v, bias, w_o, ln_g, ln_b, cw, cb, gng, l1g, l1b)


def _attention_bias(rpb, rows):
    nblk = rows // 2
    cols = jnp.arange(GRID_W)
    cstart = jnp.clip(cols - NA_COLS // 2, 0, GRID_W - NA_COLS)
    col_ok = (cols[None, :] >= cstart[:, None]) & (cols[None, :] < cstart[:, None] + NA_COLS)
    dc_idx = jnp.clip(cols[None, :] - cols[:, None], -(NA_COLS - 1), NA_COLS - 1) + NA_COLS - 1
    out = []
    for i in (0, 1, 2, nblk - 2, nblk - 1):
        p0 = min(max(i - 2, 0), nblk - KV_PAIRS)
        r = 2 * i + jnp.arange(2)
        kr = 2 * p0 + jnp.arange(2 * KV_PAIRS)
        rs = jnp.clip(r - NA_ROWS // 2, 0, rows - NA_ROWS)
        row_ok = (kr[None, :] >= rs[:, None]) & (kr[None, :] < rs[:, None] + NA_ROWS)
        dr_idx = jnp.clip(kr[None, :] - r[:, None] + NA_ROWS - 1, 0, 2 * NA_ROWS - 2)
        b = rpb[:, dr_idx[:, None, :, None], dc_idx[None, :, None, :]]
        ok = row_ok[:, None, :, None] & col_ok[None, :, None, :]
        b = jnp.where(ok[None], b, NEG_INF)
        out.append(b.reshape(N_HEADS, QBLK, KBLK))
    return jnp.stack(out).astype(_F32)


def _ffn_kernel(tiles_per_seq, x_ref, xp_ref, xn_ref, wg_ref, wv_ref, cwg_ref, cwv_ref,
                cbg_ref, cbv_ref, wd_ref, l2g_ref, l2b_ref, o_ref, xb_scr, acc_scr):
    tm = x_ref.shape[0]
    i = pl.program_id(0)
    first = (i % tiles_per_seq) == 0
    last = (i % tiles_per_seq) == tiles_per_seq - 1
    xt = x_ref[...]
    xp = xp_ref[...]
    xnx = xn_ref[...]
    xb_scr[0:HALO] = jnp.where(first, jnp.zeros_like(xp), xp).astype(_BF16)
    xb_scr[HALO:HALO + tm] = xt.astype(_BF16)
    xb_scr[HALO + tm:] = jnp.where(last, jnp.zeros_like(xnx), xnx).astype(_BF16)
    acc_scr[...] = jnp.zeros_like(acc_scr)
    ext = tm + 2 * HALO

    def conv(h, w, b):
        h_m1 = pltpu.roll(h, 1, 0)[HALO:HALO + tm]
        h_p1 = pltpu.roll(h, ext - 1, 0)[HALO:HALO + tm]
        return w[0:1] * h_m1 + w[1:2] * h[HALO:HALO + tm] + w[2:3] * h_p1 + b

    def chunk(c, carry):
        xb = xb_scr[...]
        ug = conv(_dot(xb, wg_ref[c]), cwg_ref[c], cbg_ref[c])
        uv = conv(_dot(xb, wv_ref[c]), cwv_ref[c], cbv_ref[c])
        a = (jax.nn.gelu(ug) * uv).astype(_BF16)
        acc_scr[...] += _dot(a, wd_ref[c])
        return carry

    lax.fori_loop(0, N_FF_CHUNKS, chunk, 0)
    z = ALPHA * xt + acc_scr[...]
    o_ref[...] = _layernorm(z, l2g_ref[...], l2b_ref[...])


def _ffn(x2d, t, wg, wv, cwg, cwv, cbg, cbv, wd, l2g, l2b, tm):
    n = x2d.shape[0]
    per = tm // HALO
    nhalo = n // HALO
    tok = lambda i: (i, 0)
    c2 = lambda i: (0, 0)
    c3 = lambda i: (0, 0, 0)
    return pl.pallas_call(
        functools.partial(_ffn_kernel, t // tm),
        grid=(n // tm,),
        in_specs=[
            pl.BlockSpec((tm, D_MODEL), tok),
            pl.BlockSpec((HALO, D_MODEL), lambda i: (jnp.maximum(i * per - 1, 0), 0)),
            pl.BlockSpec((HALO, D_MODEL), lambda i: (jnp.minimum((i + 1) * per, nhalo - 1), 0)),
            pl.BlockSpec(wg.shape, c3),
            pl.BlockSpec(wv.shape, c3),
            pl.BlockSpec(cwg.shape, c3),
            pl.BlockSpec(cwv.shape, c3),
            pl.BlockSpec(cbg.shape, c3),
            pl.BlockSpec(cbv.shape, c3),
            pl.BlockSpec(wd.shape, c3),
            pl.BlockSpec((1, D_MODEL), c2),
            pl.BlockSpec((1, D_MODEL), c2),
        ],
        out_specs=pl.BlockSpec((tm, D_MODEL), tok),
        out_shape=jax.ShapeDtypeStruct((n, D_MODEL), _F32),
        scratch_shapes=[
            pltpu.VMEM((tm + 2 * HALO, D_MODEL), _BF16),
            pltpu.VMEM((tm, D_MODEL), _F32),
        ],
        compiler_params=pltpu.CompilerParams(
            dimension_semantics=("arbitrary",), vmem_limit_bytes=VMEM_LIMIT_BYTES),
        name="convffn",
    )(x2d, x2d, x2d, wg, wv, cwg, cwv, cbg, cbv, wd, l2g, l2b)


def _chunked_cols(w):
    r = w.shape[0]
    return w.reshape(r, N_FF_CHUNKS, FF_CHUNK).transpose(1, 0, 2)


def _trunk(x, p, tm_in=512, tm_ffn=512):
    bsz, t, _ = x.shape
    n = bsz * t
    x2d = x.reshape(n, D_MODEL)
    bg, gg, q, v, kt = _inproj(x2d, p["ln_in_g"], p["ln_in_b"], p["w_main"], p["w_kt"], tm_in)
    r3 = lambda a: a.reshape(bsz, t, a.shape[-1])
    bias = _attention_bias(p["rpb"], t // GRID_W)
    x1 = _mixer(x, r3(bg), r3(gg), r3(q), kt.reshape(bsz, t // QBLK, ATTN_WIDTH, QBLK), r3(v),
                bias, p["w_o"], p["ln_in_g"], p["ln_in_b"], p["conv_w"], p["conv_b"],
                p["gn_g"], p["ln1_g"], p["ln1_b"])
    y = _ffn(x1.reshape(n, D_MODEL), t, p["wg"], p["wv"], p["cwg"], p["cwv"], p["cbg"],
             p["cbv"], p["wd"], p["ln2_g"], p["ln2_b"], tm_ffn)
    return y.reshape(bsz, t, D_MODEL)


def _prepare(ln_in_g, ln_in_b, w_in, conv_w, conv_b, rpb, gn_g, w_o, ln1_g, ln1_b,
             w_up, ffn_conv_w, ffn_conv_b, w_down, ln2_g, ln2_b):
    row = lambda a: a.reshape(1, -1).astype(_F32)
    c3 = 3 * CONV_WIDTH
    w_in = w_in[0]
    w_k = w_in[:, c3 + ATTN_WIDTH:c3 + 2 * ATTN_WIDTH]
    w_main = jnp.concatenate([w_in[:, :c3 + ATTN_WIDTH], w_in[:, c3 + 2 * ATTN_WIDTH:]], axis=1)
    w_up = w_up[0]
    fcw = ffn_conv_w[0]
    fcb = ffn_conv_b[0].reshape(1, -1)
    return {
        "ln_in_g": row(ln_in_g), "ln_in_b": row(ln_in_b),
        "w_main": w_main.astype(_BF16), "w_kt": w_k.T.astype(_BF16),
        "conv_w": conv_w[0].astype(_F32), "conv_b": row(conv_b[0]),
        "rpb": rpb[0].astype(_F32), "gn_g": row(gn_g[0]),
        "w_o": w_o[0].astype(_BF16), "ln1_g": row(ln1_g[0]), "ln1_b": row(ln1_b[0]),
        "wg": _chunked_cols(w_up[:, :D_FF]).astype(_BF16),
        "wv": _chunked_cols(w_up[:, D_FF:]).astype(_BF16),
        "cwg": _chunked_cols(fcw[:, :D_FF]).astype(_F32),
        "cwv": _chunked_cols(fcw[:, D_FF:]).astype(_F32),
        "cbg": _chunked_cols(fcb[:, :D_FF]).astype(_F32),
        "cbv": _chunked_cols(fcb[:, D_FF:]).astype(_F32),
        "wd": w_down[0].reshape(N_FF_CHUNKS, FF_CHUNK, D_MODEL).astype(_BF16),
        "ln2_g": row(ln2_g[0]), "ln2_b": row(ln2_b[0]),
    }


def kernel(x_prompt, x_sample, ln_in_g, ln_in_b, w_in, conv_w, conv_b, rpb, gn_g, w_o, ln1_g,
           ln1_b, w_up, ffn_conv_w, ffn_conv_b, w_down, ln2_g, ln2_b):
    p = _prepare(ln_in_g, ln_in_b, w_in, conv_w, conv_b, rpb, gn_g, w_o, ln1_g, ln1_b,
                 w_up, ffn_conv_w, ffn_conv_b, w_down, ln2_g, ln2_b)
    return (_trunk(x_prompt, p), _trunk(x_sample, p))
```

```python
import functools

import jax
import jax.numpy as jnp
import numpy as np
from jax import lax
from jax.experimental import pallas as pl
from jax.experimental.pallas import tpu as pltpu

D_MODEL = 1024
CONV_WIDTH = 512
ATTN_WIDTH = 512
N_HEADS = 8
HEAD_DIM = 64
GRID_W = 64
NA_ROWS = 8
NA_COLS = 16
D_FF = 2816
LN_EPS = 1e-5
ALPHA = 2.0 ** 0.25
NEG_INF = -1e30

QBLK = 2 * GRID_W
KV_PAIRS = 5
KBLK = KV_PAIRS * QBLK
HALO = 16
FF_CHUNK = 256
ACT_ROWS = 64
LANES = 128
N_GROUPS = 4
N_FF_CHUNKS = D_FF // FF_CHUNK
VMEM_LIMIT_BYTES = 56 * 1024 * 1024

_F32 = jnp.float32
_BF16 = jnp.bfloat16


def _layernorm(x, g, b):
    mu = jnp.mean(x, axis=-1, keepdims=True)
    xc = x - mu
    var = jnp.mean(xc * xc, axis=-1, keepdims=True)
    return xc * lax.rsqrt(var + LN_EPS) * g + b


def _rmsnorm(x, g):
    ms = jnp.mean(x * x, axis=-1, keepdims=True)
    return x * lax.rsqrt(ms + LN_EPS) * g


def _dot(a, b):
    return jnp.dot(a, b, preferred_element_type=_F32)


def _inproj_kernel(x_ref, g_ref, b_ref, w_ref, wkt_ref,
                   bg_ref, gg_ref, q_ref, v_ref, kt_ref):
    tm = x_ref.shape[0]
    xn = _layernorm(x_ref[...], g_ref[...], b_ref[...])
    xb = xn.astype(_BF16)
    c1, c2, c3 = CONV_WIDTH, 2 * CONV_WIDTH, 3 * CONV_WIDTH
    pa = _dot(xb, w_ref[:, 0:c3])
    bg_ref[...] = pa[:, 0:c1].astype(_BF16)
    gg_ref[...] = (pa[:, c1:c2] * pa[:, c2:c3]).astype(_BF16)
    q = _dot(xb, w_ref[:, c3:c3 + ATTN_WIDTH])
    q_ref[...] = (q * (HEAD_DIM ** -0.5)).astype(_BF16)
    v_ref[...] = _dot(xb, w_ref[:, c3 + ATTN_WIDTH:c3 + 2 * ATTN_WIDTH]).astype(_BF16)
    kt = lax.dot_general(wkt_ref[...], xb, (((1,), (1,)), ((), ())),
                         preferred_element_type=_F32)
    for i in range(tm // QBLK):
        kt_ref[i] = kt[:, i * QBLK:(i + 1) * QBLK].astype(_BF16)


def _inproj(x2d, ln_g, ln_b, w_main, w_kt, tm):
    n = x2d.shape[0]
    const = lambda i: (0, 0)
    tok = lambda i: (i, 0)
    outs = pl.pallas_call(
        _inproj_kernel,
        grid=(n // tm,),
        in_specs=[
            pl.BlockSpec((tm, D_MODEL), tok),
            pl.BlockSpec((1, D_MODEL), const),
            pl.BlockSpec((1, D_MODEL), const),
            pl.BlockSpec(w_main.shape, const),
            pl.BlockSpec(w_kt.shape, const),
        ],
        out_specs=[
            pl.BlockSpec((tm, CONV_WIDTH), tok),
            pl.BlockSpec((tm, CONV_WIDTH), tok),
            pl.BlockSpec((tm, ATTN_WIDTH), tok),
            pl.BlockSpec((tm, ATTN_WIDTH), tok),
            pl.BlockSpec((tm // QBLK, ATTN_WIDTH, QBLK), lambda i: (i, 0, 0)),
        ],
        out_shape=[
            jax.ShapeDtypeStruct((n, CONV_WIDTH), _BF16),
            jax.ShapeDtypeStruct((n, CONV_WIDTH), _BF16),
            jax.ShapeDtypeStruct((n, ATTN_WIDTH), _BF16),
            jax.ShapeDtypeStruct((n, ATTN_WIDTH), _BF16),
            jax.ShapeDtypeStruct((n // QBLK, ATTN_WIDTH, QBLK), _BF16),
        ],
        compiler_params=pltpu.CompilerParams(
            dimension_semantics=("arbitrary",), vmem_limit_bytes=VMEM_LIMIT_BYTES),
        name="inproj",
    )(x2d, ln_g, ln_b, w_main, w_kt)
    return outs


def _mixer_kernel(nblk, x_ref, bg_ref, gg_ref, gp_ref, gn_ref, q_ref, kt_ref, v_ref,
                  bias_ref, wo_ref, lng_ref, lnb_ref, cw_ref, cb_ref, gng_ref,
                  l1g_ref, l1b_ref, o_ref):
    i = pl.program_id(1)
    p0 = jnp.clip(i - 2, 0, nblk - KV_PAIRS)

    lane = lax.broadcasted_iota(jnp.int32, (QBLK, 2 * HEAD_DIM), 1)
    low = lane < HEAD_DIM
    q = q_ref[0]
    row0 = pl.multiple_of(p0 * QBLK, QBLK)
    ys = []
    for j in range(N_HEADS // 2):
        cs = slice(j * 2 * HEAD_DIM, (j + 1) * 2 * HEAD_DIM)
        qp = q[:, cs]
        ktp = jnp.concatenate([kt_ref[0, p0 + t, cs, :] for t in range(KV_PAIRS)], axis=1)
        vp = v_ref[0, pl.ds(row0, KBLK), cs]
        outs = []
        for a in range(2):
            sel = low if a == 0 else jnp.logical_not(low)
            qa = jnp.where(sel, qp, jnp.zeros_like(qp))
            s = _dot(qa, ktp) + bias_ref[0, 2 * j + a]
            m = jnp.max(s, axis=-1, keepdims=True)
            p = jnp.exp(s - m)
            l = jnp.sum(p, axis=-1, keepdims=True)
            outs.append(_dot(p.astype(_BF16), vp) / l)
        ys.append(jnp.where(low, outs[0], outs[1]))
    y_attn = jnp.concatenate(ys, axis=1)

    g = gg_ref[0].astype(_F32)
    prev_row = gp_ref[0].astype(_F32)[HALO - 1:HALO, :]
    next_row = gn_ref[0].astype(_F32)[0:1, :]
    prev_row = jnp.where(i > 0, prev_row, jnp.zeros_like(prev_row))
    next_row = jnp.where(i < nblk - 1, next_row, jnp.zeros_like(next_row))
    row = lax.broadcasted_iota(jnp.int32, g.shape, 0)
    g_m1 = jnp.where(row == 0, prev_row, pltpu.roll(g, 1, 0))
    g_p1 = jnp.where(row == QBLK - 1, next_row, pltpu.roll(g, QBLK - 1, 0))
    cw = cw_ref[...]
    conv = cw[0:1] * g_m1 + cw[1:2] * g + cw[2:3] * g_p1 + cb_ref[...]
    y_conv = bg_ref[0].astype(_F32) * conv

    gng = gng_ref[...]
    y = jnp.concatenate([_rmsnorm(y_conv, gng[:, :CONV_WIDTH]),
                         _rmsnorm(y_attn, gng[:, CONV_WIDTH:])], axis=1).astype(_BF16)
    xn = _layernorm(x_ref[0], lng_ref[...], lnb_ref[...])
    z = ALPHA * xn + _dot(y, wo_ref[...])
    o_ref[0] = _layernorm(z, l1g_ref[...], l1b_ref[...])


def _mixer(x3d, bg, gg, q, kt, v, bias, w_o, ln_g, ln_b, cw, cb, gng, l1g, l1b):
    bsz, t, _ = x3d.shape
    nblk = t // QBLK
    nhalo = t // HALO
    per = QBLK // HALO

    def bias_type(i):
        return jnp.where(i == 0, 0, jnp.where(i == 1, 1, jnp.where(
            i == nblk - 2, 3, jnp.where(i == nblk - 1, 4, 2))))

    tok = lambda b, i: (b, i, 0)
    c2 = lambda b, i: (0, 0)
    return pl.pallas_call(
        functools.partial(_mixer_kernel, nblk),
        grid=(bsz, nblk),
        in_specs=[
            pl.BlockSpec((1, QBLK, D_MODEL), tok),
            pl.BlockSpec((1, QBLK, CONV_WIDTH), tok),
            pl.BlockSpec((1, QBLK, CONV_WIDTH), tok),
            pl.BlockSpec((1, HALO, CONV_WIDTH), lambda b, i: (b, jnp.maximum(i * per - 1, 0), 0)),
            pl.BlockSpec((1, HALO, CONV_WIDTH),
                         lambda b, i: (b, jnp.minimum((i + 1) * per, nhalo - 1), 0)),
            pl.BlockSpec((1, QBLK, ATTN_WIDTH), tok),
            pl.BlockSpec((1, nblk, ATTN_WIDTH, QBLK), lambda b, i: (b, 0, 0, 0)),
            pl.BlockSpec((1, t, ATTN_WIDTH), lambda b, i: (b, 0, 0)),
            pl.BlockSpec((1, N_HEADS, QBLK, KBLK), lambda b, i: (bias_type(i), 0, 0, 0)),
            pl.BlockSpec((D_MODEL, D_MODEL), c2),
            pl.BlockSpec((1, D_MODEL), c2),
            pl.BlockSpec((1, D_MODEL), c2),
            pl.BlockSpec((3, CONV_WIDTH), c2),
            pl.BlockSpec((1, CONV_WIDTH), c2),
            pl.BlockSpec((1, D_MODEL), c2),
            pl.BlockSpec((1, D_MODEL), c2),
            pl.BlockSpec((1, D_MODEL), c2),
        ],
        out_specs=pl.BlockSpec((1, QBLK, D_MODEL), tok),
        out_shape=jax.ShapeDtypeStruct((bsz, t, D_MODEL), _F32),
        compiler_params=pltpu.CompilerParams(
            dimension_semantics=("arbitrary", "arbitrary"), vmem_limit_bytes=VMEM_LIMIT_BYTES),
        name="mixer",
    )(x3d, bg, gg, gg, gg, q, kt, v, bias, w_o, ln_g, ln_b, cw, cb, gng, l1g, l1b)


def _attention_bias(rpb):
    rows = 4 * KV_PAIRS
    nblk = rows // 2
    cols = np.arange(GRID_W)
    cstart = np.clip(cols - NA_COLS // 2, 0, GRID_W - NA_COLS)
    col_ok = (cols[None, :] >= cstart[:, None]) & (cols[None, :] < cstart[:, None] + NA_COLS)
    pad = GRID_W - NA_COLS
    rp = jnp.pad(rpb.astype(_F32), ((0, 0), (0, 0), (pad, pad)))
    toep = jnp.stack([rp[:, :, GRID_W - 1 - c:2 * GRID_W - 1 - c] for c in range(GRID_W)], axis=2)
    bfull = jnp.where(col_ok[None, None], toep, NEG_INF)
    masked = jnp.full((N_HEADS, GRID_W, GRID_W), NEG_INF, _F32)
    out = []
    for i in (0, 1, 2, nblk - 2, nblk - 1):
        p0 = min(max(i - 2, 0), nblk - KV_PAIRS)
        blocks = []
        for r in (2 * i, 2 * i + 1):
            rs = min(max(r - NA_ROWS // 2, 0), rows - NA_ROWS)
            tiles = []
            for kr in range(2 * p0, 2 * p0 + 2 * KV_PAIRS):
                inside = rs <= kr < rs + NA_ROWS
                tiles.append(bfull[:, kr - r + NA_ROWS - 1] if inside else masked)
            blocks.append(jnp.concatenate(tiles, axis=2))
        out.append(jnp.concatenate(blocks, axis=1))
    return jnp.stack(out)


def _ffn_kernel(tiles_per_seq, x_ref, xp_ref, xn_ref, wu_ref, cw_ref, cb_ref, wd_ref,
                l2g_ref, l2b_ref, o_ref, xb_scr, h0_scr, h1_scr, a0_scr, a1_scr, acc_scr):
    tm = x_ref.shape[0]
    h_scr = (h0_scr, h1_scr)
    a_scr = (a0_scr, a1_scr)
    i = pl.program_id(0)
    first = (i % tiles_per_seq) == 0
    last = (i % tiles_per_seq) == tiles_per_seq - 1
    xt = x_ref[...]
    xp = xp_ref[...]
    xnx = xn_ref[...]
    xb_scr[0:HALO] = jnp.where(first, jnp.zeros_like(xp), xp).astype(_BF16)
    xb_scr[HALO:HALO + tm] = xt.astype(_BF16)
    xb_scr[HALO + tm:] = jnp.where(last, jnp.zeros_like(xnx), xnx).astype(_BF16)
    acc_scr[...] = ALPHA * xt

    rows_g = tm // N_GROUPS

    def up(c, slot, g):
        lo = 0 if g == 0 else HALO + g * rows_g
        hi = HALO + (g + 1) * rows_g + (HALO if g == N_GROUPS - 1 else 0)
        h_scr[slot][lo:hi, :] = _dot(xb_scr[lo:hi, :], wu_ref[c])

    def act(c, slot, g):
        w = cw_ref[c]
        b = cb_ref[c]
        pad = 8
        for r in range(g * rows_g, (g + 1) * rows_g, ACT_ROWS):
            for j in range(0, FF_CHUNK, LANES):
                def conv(col):
                    cs = slice(col, col + LANES)
                    hs = h_scr[slot][pl.ds(HALO + r - pad, ACT_ROWS + 2 * pad), cs]
                    h_m1 = pltpu.roll(hs, 1, 0)[pad:pad + ACT_ROWS]
                    h_p1 = pltpu.roll(hs, ACT_ROWS + 2 * pad - 1, 0)[pad:pad + ACT_ROWS]
                    return (w[0:1, cs] * h_m1 + w[1:2, cs] * hs[pad:pad + ACT_ROWS]
                            + w[2:3, cs] * h_p1 + b[:, cs])
                a = jax.nn.gelu(conv(j)) * conv(FF_CHUNK + j)
                a_scr[slot][r:r + ACT_ROWS, j:j + LANES] = a.astype(_BF16)

    def down(c, slot, g):
        rs = slice(g * rows_g, (g + 1) * rows_g)
        acc_scr[rs, :] += _dot(a_scr[slot][rs, :], wd_ref[c])

    def step(c, slot, do_up=True, do_act=True, do_down=True):
        for g in range(N_GROUPS):
            if do_up:
                up(c + 1, 1 - slot, g)
            if do_act:
                act(c, slot, g)
            if do_down:
                down(c - 1, 1 - slot, g)

    step(-1, 1, do_act=False, do_down=False)
    step(0, 0, do_down=False)

    def pair(k, carry):
        c = 2 * k + 1
        step(c, 1)
        step(c + 1, 0)
        return carry

    lax.fori_loop(0, (N_FF_CHUNKS - 3) // 2, pair, 0)
    step(N_FF_CHUNKS - 2, 1)
    step(N_FF_CHUNKS - 1, 0, do_up=False)
    step(N_FF_CHUNKS, 1, do_up=False, do_act=False)
    o_ref[...] = _layernorm(acc_scr[...], l2g_ref[...], l2b_ref[...])


def _ffn(x2d, t, wu, cw, cb, wd, l2g, l2b, tm):
    assert N_FF_CHUNKS % 2 == 1 and N_FF_CHUNKS >= 3
    n = x2d.shape[0]
    per = tm // HALO
    nhalo = n // HALO
    tok = lambda i: (i, 0)
    c2 = lambda i: (0, 0)
    c3 = lambda i: (0, 0, 0)
    return pl.pallas_call(
        functools.partial(_ffn_kernel, t // tm),
        grid=(n // tm,),
        in_specs=[
            pl.BlockSpec((tm, D_MODEL), tok),
            pl.BlockSpec((HALO, D_MODEL), lambda i: (jnp.maximum(i * per - 1, 0), 0)),
            pl.BlockSpec((HALO, D_MODEL), lambda i: (jnp.minimum((i + 1) * per, nhalo - 1), 0)),
            pl.BlockSpec(wu.shape, c3),
            pl.BlockSpec(cw.shape, c3),
            pl.BlockSpec(cb.shape, c3),
            pl.BlockSpec(wd.shape, c3),
            pl.BlockSpec((1, D_MODEL), c2),
            pl.BlockSpec((1, D_MODEL), c2),
        ],
        out_specs=pl.BlockSpec((tm, D_MODEL), tok),
        out_shape=jax.ShapeDtypeStruct((n, D_MODEL), _F32),
        scratch_shapes=[
            pltpu.VMEM((tm + 2 * HALO, D_MODEL), _BF16),
            pltpu.VMEM((tm + 2 * HALO, 2 * FF_CHUNK), _F32),
            pltpu.VMEM((tm + 2 * HALO, 2 * FF_CHUNK), _F32),
            pltpu.VMEM((tm, FF_CHUNK), _BF16),
            pltpu.VMEM((tm, FF_CHUNK), _BF16),
            pltpu.VMEM((tm, D_MODEL), _F32),
        ],
        compiler_params=pltpu.CompilerParams(
            dimension_semantics=("arbitrary",), vmem_limit_bytes=VMEM_LIMIT_BYTES),
        name="convffn",
    )(x2d, x2d, x2d, wu, cw, cb, wd, l2g, l2b)


def _gate_val_chunks(w):
    r = w.shape[0]
    w = w.reshape(r, 2, N_FF_CHUNKS, FF_CHUNK).transpose(2, 0, 1, 3)
    return w.reshape(N_FF_CHUNKS, r, 2 * FF_CHUNK)


def _trunk(x, p, tm_in=512, tm_ffn=512):
    bsz, t, _ = x.shape
    n = bsz * t
    x2d = x.reshape(n, D_MODEL)
    bg, gg, q, v, kt = _inproj(x2d, p["ln_in_g"], p["ln_in_b"], p["w_main"], p["w_kt"], tm_in)
    r3 = lambda a: a.reshape(bsz, t, a.shape[-1])
    assert t % QBLK == 0 and t // QBLK >= KV_PAIRS and t % tm_ffn == 0 and n % tm_in == 0
    x1 = _mixer(x, r3(bg), r3(gg), r3(q), kt.reshape(bsz, t // QBLK, ATTN_WIDTH, QBLK), r3(v),
                p["bias"], p["w_o"], p["ln_in_g"], p["ln_in_b"], p["conv_w"], p["conv_b"],
                p["gn_g"], p["ln1_g"], p["ln1_b"])
    y = _ffn(x1.reshape(n, D_MODEL), t, p["wu"], p["cw"], p["cb"], p["wd"],
             p["ln2_g"], p["ln2_b"], tm_ffn)
    return y.reshape(bsz, t, D_MODEL)


def _prepare(ln_in_g, ln_in_b, w_in, conv_w, conv_b, rpb, gn_g, w_o, ln1_g, ln1_b,
             w_up, ffn_conv_w, ffn_conv_b, w_down, ln2_g, ln2_b):
    row = lambda a: a.reshape(1, -1).astype(_F32)
    c3 = 3 * CONV_WIDTH
    w_in = w_in[0]
    w_k = w_in[:, c3 + ATTN_WIDTH:c3 + 2 * ATTN_WIDTH]
    w_main = jnp.concatenate([w_in[:, :c3 + ATTN_WIDTH], w_in[:, c3 + 2 * ATTN_WIDTH:]], axis=1)
    w_up = w_up[0]
    fcw = ffn_conv_w[0]
    fcb = ffn_conv_b[0].reshape(1, -1)
    return {
        "ln_in_g": row(ln_in_g), "ln_in_b": row(ln_in_b),
        "w_main": w_main.astype(_BF16), "w_kt": w_k.T.astype(_BF16),
        "conv_w": conv_w[0].astype(_F32), "conv_b": row(conv_b[0]),
        "bias": _attention_bias(rpb[0]), "gn_g": row(gn_g[0]),
        "w_o": w_o[0].astype(_BF16), "ln1_g": row(ln1_g[0]), "ln1_b": row(ln1_b[0]),
        "wu": _gate_val_chunks(w_up).astype(_BF16),
        "cw": _gate_val_chunks(fcw).astype(_F32),
        "cb": _gate_val_chunks(fcb).astype(_F32),
        "wd": w_down[0].reshape(N_FF_CHUNKS, FF_CHUNK, D_MODEL).astype(_BF16),
        "ln2_g": row(ln2_g[0]), "ln2_b": row(ln2_b[0]),
    }


def kernel(x_prompt, x_sample, ln_in_g, ln_in_b, w_in, conv_w, conv_b, rpb, gn_g, w_o, ln1_g,
           ln1_b, w_up, ffn_conv_w, ffn_conv_b, w_down, ln2_g, ln2_b):
    p = _prepare(ln_in_g, ln_in_b, w_in, conv_w, conv_b, rpb, gn_g, w_o, ln1_g, ln1_b,
                 w_up, ffn_conv_w, ffn_conv_b, w_down, ln2_g, ln2_b)
    return (_trunk(x_prompt, p), _trunk(x_sample, p))
```

```python
import functools

import jax
import jax.numpy as jnp
import numpy as np
from jax import lax
from jax.experimental import pallas as pl
from jax.experimental.pallas import tpu as pltpu

D_MODEL = 1024
CONV_WIDTH = 512
ATTN_WIDTH = 512
N_HEADS = 8
HEAD_DIM = 64
GRID_W = 64
NA_ROWS = 8
NA_COLS = 16
D_FF = 2816
LN_EPS = 1e-5
ALPHA = 2.0 ** 0.25
NEG_INF = -1e30

QBLK = 2 * GRID_W
KV_PAIRS = 5
KBLK = KV_PAIRS * QBLK
HALO = 16
FF_CHUNK = 256
ACT_ROWS = 64
LANES = 128
N_GROUPS = 4
OPROJ_GROUPS = 2
MIX_BLOCKS = 2
N_FF_CHUNKS = D_FF // FF_CHUNK
VMEM_LIMIT_BYTES = 56 * 1024 * 1024

_F32 = jnp.float32
_BF16 = jnp.bfloat16


def _layernorm(x, g, b):
    mu = jnp.mean(x, axis=-1, keepdims=True)
    xc = x - mu
    var = jnp.mean(xc * xc, axis=-1, keepdims=True)
    return xc * lax.rsqrt(var + LN_EPS) * g + b


def _rmsnorm(x, g):
    ms = jnp.mean(x * x, axis=-1, keepdims=True)
    return x * lax.rsqrt(ms + LN_EPS) * g


def _dot(a, b):
    return jnp.dot(a, b, preferred_element_type=_F32)


def _inproj_kernel(x_ref, g_ref, b_ref, w_ref, wkt_ref,
                   bg_ref, gg_ref, q_ref, v_ref, kt_ref, xres_ref):
    tm = x_ref.shape[0]
    xn = _layernorm(x_ref[...], g_ref[...], b_ref[...])
    xres_ref[...] = ALPHA * xn
    xb = xn.astype(_BF16)
    c1, c2, c3 = CONV_WIDTH, 2 * CONV_WIDTH, 3 * CONV_WIDTH
    pa = _dot(xb, w_ref[:, 0:c3])
    bg_ref[...] = pa[:, 0:c1].astype(_BF16)
    gg_ref[...] = (pa[:, c1:c2] * pa[:, c2:c3]).astype(_BF16)
    q = _dot(xb, w_ref[:, c3:c3 + ATTN_WIDTH])
    q_ref[...] = (q * (HEAD_DIM ** -0.5)).astype(_BF16)
    v_ref[...] = _dot(xb, w_ref[:, c3 + ATTN_WIDTH:c3 + 2 * ATTN_WIDTH]).astype(_BF16)
    kt = lax.dot_general(wkt_ref[...], xb, (((1,), (1,)), ((), ())),
                         preferred_element_type=_F32)
    for i in range(tm // QBLK):
        kt_ref[i] = kt[:, i * QBLK:(i + 1) * QBLK].astype(_BF16)


def _inproj(x2d, ln_g, ln_b, w_main, w_kt, tm):
    n = x2d.shape[0]
    const = lambda i: (0, 0)
    tok = lambda i: (i, 0)
    outs = pl.pallas_call(
        _inproj_kernel,
        grid=(n // tm,),
        in_specs=[
            pl.BlockSpec((tm, D_MODEL), tok),
            pl.BlockSpec((1, D_MODEL), const),
            pl.BlockSpec((1, D_MODEL), const),
            pl.BlockSpec(w_main.shape, const),
            pl.BlockSpec(w_kt.shape, const),
        ],
        out_specs=[
            pl.BlockSpec((tm, CONV_WIDTH), tok),
            pl.BlockSpec((tm, CONV_WIDTH), tok),
            pl.BlockSpec((tm, ATTN_WIDTH), tok),
            pl.BlockSpec((tm, ATTN_WIDTH), tok),
            pl.BlockSpec((tm // QBLK, ATTN_WIDTH, QBLK), lambda i: (i, 0, 0)),
            pl.BlockSpec((tm, D_MODEL), tok),
        ],
        out_shape=[
            jax.ShapeDtypeStruct((n, CONV_WIDTH), _BF16),
            jax.ShapeDtypeStruct((n, CONV_WIDTH), _BF16),
            jax.ShapeDtypeStruct((n, ATTN_WIDTH), _BF16),
            jax.ShapeDtypeStruct((n, ATTN_WIDTH), _BF16),
            jax.ShapeDtypeStruct((n // QBLK, ATTN_WIDTH, QBLK), _BF16),
            jax.ShapeDtypeStruct((n, D_MODEL), _F32),
        ],
        compiler_params=pltpu.CompilerParams(
            dimension_semantics=("arbitrary",), vmem_limit_bytes=VMEM_LIMIT_BYTES),
        name="inproj",
    )(x2d, ln_g, ln_b, w_main, w_kt)
    return outs


def _mixer_kernel(nblk, bg_ref, gg_ref, gp_ref, gn_ref, q_ref, kt_ref, v_ref,
                  bias_ref, cw_ref, cb_ref, gng_ref, y_ref):
    step = pl.program_id(1)
    nstep = nblk // MIX_BLOCKS
    tq = MIX_BLOCKS * QBLK
    lane = lax.broadcasted_iota(jnp.int32, (QBLK, 2 * HEAD_DIM), 1)
    low = lane < HEAD_DIM

    def head_cols(h):
        j = h // 2
        return slice(j * 2 * HEAD_DIM, (j + 1) * 2 * HEAD_DIM)

    def attention(sub):
        blk = step * MIX_BLOCKS + sub
        p0 = jnp.clip(blk - 2, 0, nblk - KV_PAIRS)
        row0 = pl.multiple_of(p0 * QBLK, QBLK)
        btype = jnp.where(blk == 0, 0, jnp.where(blk == 1, 1, jnp.where(
            blk == nblk - 2, 3, jnp.where(blk == nblk - 1, 4, 2))))
        qrows = slice(sub * QBLK, (sub + 1) * QBLK)

        def scores(h):
            cs = head_cols(h)
            qp = q_ref[0, qrows, cs]
            sel = low if h % 2 == 0 else jnp.logical_not(low)
            qa = jnp.where(sel, qp, jnp.zeros_like(qp))
            ktp = jnp.concatenate([kt_ref[0, p0 + t, cs, :] for t in range(KV_PAIRS)], axis=1)
            return _dot(qa, ktp) + bias_ref[btype, h]

        def softmax_num(s):
            m = jnp.max(s, axis=-1, keepdims=True)
            p = jnp.exp(s - m)
            return p.astype(_BF16), jnp.sum(p, axis=-1, keepdims=True)

        def weighted(h, p, l):
            return _dot(p, v_ref[0, pl.ds(row0, KBLK), head_cols(h)]) / l

        s = {0: scores(0), 1: scores(1)}
        outs = {}
        for h in range(N_HEADS):
            p, l = softmax_num(s.pop(h))
            if h + 2 < N_HEADS:
                s[h + 2] = scores(h + 2)
            outs[h] = weighted(h, p, l)
        y_attn = jnp.concatenate(
            [jnp.where(low, outs[2 * j], outs[2 * j + 1]) for j in range(N_HEADS // 2)], axis=1)
        y_ref[0, qrows, CONV_WIDTH:] = _rmsnorm(y_attn, gng_ref[:, CONV_WIDTH:]).astype(_BF16)

    def conv_mixer():
        g = gg_ref[0].astype(_F32)
        prev_row = gp_ref[0].astype(_F32)[HALO - 1:HALO, :]
        next_row = gn_ref[0].astype(_F32)[0:1, :]
        prev_row = jnp.where(step > 0, prev_row, jnp.zeros_like(prev_row))
        next_row = jnp.where(step < nstep - 1, next_row, jnp.zeros_like(next_row))
        row = lax.broadcasted_iota(jnp.int32, g.shape, 0)
        g_m1 = jnp.where(row == 0, prev_row, pltpu.roll(g, 1, 0))
        g_p1 = jnp.where(row == tq - 1, next_row, pltpu.roll(g, tq - 1, 0))
        cw = cw_ref[...]
        conv = cw[0:1] * g_m1 + cw[1:2] * g + cw[2:3] * g_p1 + cb_ref[...]
        y_conv = bg_ref[0].astype(_F32) * conv
        y_ref[0, :, 0:CONV_WIDTH] = _rmsnorm(y_conv, gng_ref[:, :CONV_WIDTH]).astype(_BF16)

    conv_mixer()
    for sub in range(MIX_BLOCKS):
        attention(sub)


def _mixer(bg, gg, q, kt, v, bias, cw, cb, gng):
    bsz, t, _ = q.shape
    nblk = t // QBLK
    tq = MIX_BLOCKS * QBLK
    nhalo = t // HALO
    per = tq // HALO
    tok = lambda b, i: (b, i, 0)
    c2 = lambda b, i: (0, 0)
    return pl.pallas_call(
        functools.partial(_mixer_kernel, nblk),
        grid=(bsz, nblk // MIX_BLOCKS),
        in_specs=[
            pl.BlockSpec((1, tq, CONV_WIDTH), tok),
            pl.BlockSpec((1, tq, CONV_WIDTH), tok),
            pl.BlockSpec((1, HALO, CONV_WIDTH), lambda b, i: (b, jnp.maximum(i * per - 1, 0), 0)),
            pl.BlockSpec((1, HALO, CONV_WIDTH),
                         lambda b, i: (b, jnp.minimum((i + 1) * per, nhalo - 1), 0)),
            pl.BlockSpec((1, tq, ATTN_WIDTH), tok),
            pl.BlockSpec((1, nblk, ATTN_WIDTH, QBLK), lambda b, i: (b, 0, 0, 0)),
            pl.BlockSpec((1, t, ATTN_WIDTH), lambda b, i: (b, 0, 0)),
            pl.BlockSpec(bias.shape, lambda b, i: (0, 0, 0, 0), pipeline_mode=pl.Buffered(1)),
            pl.BlockSpec((3, CONV_WIDTH), c2),
            pl.BlockSpec((1, CONV_WIDTH), c2),
            pl.BlockSpec((1, D_MODEL), c2),
        ],
        out_specs=pl.BlockSpec((1, tq, D_MODEL), tok),
        out_shape=jax.ShapeDtypeStruct((bsz, t, D_MODEL), _BF16),
        compiler_params=pltpu.CompilerParams(
            dimension_semantics=("arbitrary", "arbitrary"), vmem_limit_bytes=VMEM_LIMIT_BYTES),
        name="mixer",
    )(bg, gg, gg, gg, q, kt, v, bias, cw, cb, gng)


def _oproj_kernel(y_ref, xres_ref, wo_ref, g_ref, b_ref, o_ref):
    tm = y_ref.shape[0]
    rows = tm // OPROJ_GROUPS
    for r in range(0, tm, rows):
        z = xres_ref[r:r + rows, :] + _dot(y_ref[r:r + rows, :], wo_ref[...])
        o_ref[r:r + rows, :] = _layernorm(z, g_ref[...], b_ref[...])


def _oproj(y2d, xres, w_o, l1g, l1b, tm):
    n = y2d.shape[0]
    tok = lambda i: (i, 0)
    c2 = lambda i: (0, 0)
    return pl.pallas_call(
        _oproj_kernel,
        grid=(n // tm,),
        in_specs=[
            pl.BlockSpec((tm, D_MODEL), tok),
            pl.BlockSpec((tm, D_MODEL), tok),
            pl.BlockSpec((D_MODEL, D_MODEL), c2),
            pl.BlockSpec((1, D_MODEL), c2),
            pl.BlockSpec((1, D_MODEL), c2),
        ],
        out_specs=pl.BlockSpec((tm, D_MODEL), tok),
        out_shape=jax.ShapeDtypeStruct((n, D_MODEL), _F32),
        compiler_params=pltpu.CompilerParams(
            dimension_semantics=("arbitrary",), vmem_limit_bytes=VMEM_LIMIT_BYTES),
        name="oproj",
    )(y2d, xres, w_o, l1g, l1b)


def _attention_bias(rpb):
    rows = 4 * KV_PAIRS
    nblk = rows // 2
    cols = np.arange(GRID_W)
    cstart = np.clip(cols - NA_COLS // 2, 0, GRID_W - NA_COLS)
    col_ok = (cols[None, :] >= cstart[:, None]) & (cols[None, :] < cstart[:, None] + NA_COLS)
    pad = GRID_W - NA_COLS
    rp = jnp.pad(rpb.astype(_F32), ((0, 0), (0, 0), (pad, pad)))
    toep = jnp.stack([rp[:, :, GRID_W - 1 - c:2 * GRID_W - 1 - c] for c in range(GRID_W)], axis=2)
    bfull = jnp.where(col_ok[None, None], toep, NEG_INF)
    masked = jnp.full((N_HEADS, GRID_W, GRID_W), NEG_INF, _F32)
    out = []
    for i in (0, 1, 2, nblk - 2, nblk - 1):
        p0 = min(max(i - 2, 0), nblk - KV_PAIRS)
        blocks = []
        for r in (2 * i, 2 * i + 1):
            rs = min(max(r - NA_ROWS // 2, 0), rows - NA_ROWS)
            tiles = []
            for kr in range(2 * p0, 2 * p0 + 2 * KV_PAIRS):
                inside = rs <= kr < rs + NA_ROWS
                tiles.append(bfull[:, kr - r + NA_ROWS - 1] if inside else masked)
            blocks.append(jnp.concatenate(tiles, axis=2))
        out.append(jnp.concatenate(blocks, axis=1))
    return jnp.stack(out)


def _ffn_kernel(tiles_per_seq, x_ref, xp_ref, xn_ref, wu_ref, cw_ref, cb_ref, wd_ref,
                l2g_ref, l2b_ref, o_ref, xb_scr, h0_scr, h1_scr, a0_scr, a1_scr, acc_scr):
    tm = x_ref.shape[0]
    h_scr = (h0_scr, h1_scr)
    a_scr = (a0_scr, a1_scr)
    i = pl.program_id(0)
    first = (i % tiles_per_seq) == 0
    last = (i % tiles_per_seq) == tiles_per_seq - 1
    xt = x_ref[...]
    xp = xp_ref[...]
    xnx = xn_ref[...]
    xb_scr[0:HALO] = jnp.where(first, jnp.zeros_like(xp), xp).astype(_BF16)
    xb_scr[HALO:HALO + tm] = xt.astype(_BF16)
    xb_scr[HALO + tm:] = jnp.where(last, jnp.zeros_like(xnx), xnx).astype(_BF16)
    acc_scr[...] = ALPHA * xt

    rows_g = tm // N_GROUPS

    def up(c, slot, g):
        lo = 0 if g == 0 else HALO + g * rows_g
        hi = HALO + (g + 1) * rows_g + (HALO if g == N_GROUPS - 1 else 0)
        h_scr[slot][lo:hi, :] = _dot(xb_scr[lo:hi, :], wu_ref[c])

    def act(c, slot, g):
        w = cw_ref[c]
        b = cb_ref[c]
        pad = 8
        for r in range(g * rows_g, (g + 1) * rows_g, ACT_ROWS):
            for j in range(0, FF_CHUNK, LANES):
                def conv(col):
                    cs = slice(col, col + LANES)
                    hs = h_scr[slot][pl.ds(HALO + r - pad, ACT_ROWS + 2 * pad), cs]
                    h_m1 = pltpu.roll(hs, 1, 0)[pad:pad + ACT_ROWS]
                    h_p1 = pltpu.roll(hs, ACT_ROWS + 2 * pad - 1, 0)[pad:pad + ACT_ROWS]
                    return (w[0:1, cs] * h_m1 + w[1:2, cs] * hs[pad:pad + ACT_ROWS]
                            + w[2:3, cs] * h_p1 + b[:, cs])
                a = jax.nn.gelu(conv(j)) * conv(FF_CHUNK + j)
                a_scr[slot][r:r + ACT_ROWS, j:j + LANES] = a.astype(_BF16)

    def down(c, slot, g):
        rs = slice(g * rows_g, (g + 1) * rows_g)
        acc_scr[rs, :] += _dot(a_scr[slot][rs, :], wd_ref[c])

    def step(c, do_up=True, do_act=True, do_down=True):
        slot = c % 2
        for g in range(N_GROUPS):
            if do_up:
                up(c + 1, 1 - slot, g)
            if do_act:
                act(c, slot, g)
            if do_down:
                down(c - 1, 1 - slot, g)

    step(-1, do_act=False, do_down=False)
    step(0, do_down=False)
    for c in range(1, N_FF_CHUNKS - 1):
        step(c)
    step(N_FF_CHUNKS - 1, do_up=False)
    step(N_FF_CHUNKS, do_up=False, do_act=False)
    o_ref[...] = _layernorm(acc_scr[...], l2g_ref[...], l2b_ref[...])


def _ffn(x2d, t, wu, cw, cb, wd, l2g, l2b, tm):
    n = x2d.shape[0]
    per = tm // HALO
    nhalo = n // HALO
    tok = lambda i: (i, 0)
    c2 = lambda i: (0, 0)
    c3 = lambda i: (0, 0, 0)
    return pl.pallas_call(
        functools.partial(_ffn_kernel, t // tm),
        grid=(n // tm,),
        in_specs=[
            pl.BlockSpec((tm, D_MODEL), tok),
            pl.BlockSpec((HALO, D_MODEL), lambda i: (jnp.maximum(i * per - 1, 0), 0)),
            pl.BlockSpec((HALO, D_MODEL), lambda i: (jnp.minimum((i + 1) * per, nhalo - 1), 0)),
            pl.BlockSpec(wu.shape, c3),
            pl.BlockSpec(cw.shape, c3),
            pl.BlockSpec(cb.shape, c3),
            pl.BlockSpec(wd.shape, c3),
            pl.BlockSpec((1, D_MODEL), c2),
            pl.BlockSpec((1, D_MODEL), c2),
        ],
        out_specs=pl.BlockSpec((tm, D_MODEL), tok),
        out_shape=jax.ShapeDtypeStruct((n, D_MODEL), _F32),
        scratch_shapes=[
            pltpu.VMEM((tm + 2 * HALO, D_MODEL), _BF16),
            pltpu.VMEM((tm + 2 * HALO, 2 * FF_CHUNK), _F32),
            pltpu.VMEM((tm + 2 * HALO, 2 * FF_CHUNK), _F32),
            pltpu.VMEM((tm, FF_CHUNK), _BF16),
            pltpu.VMEM((tm, FF_CHUNK), _BF16),
            pltpu.VMEM((tm, D_MODEL), _F32),
        ],
        compiler_params=pltpu.CompilerParams(
            dimension_semantics=("arbitrary",), vmem_limit_bytes=VMEM_LIMIT_BYTES),
        name="convffn",
    )(x2d, x2d, x2d, wu, cw, cb, wd, l2g, l2b)


def _gate_val_chunks(w):
    r = w.shape[0]
    w = w.reshape(r, 2, N_FF_CHUNKS, FF_CHUNK).transpose(2, 0, 1, 3)
    return w.reshape(N_FF_CHUNKS, r, 2 * FF_CHUNK)


def _trunk(x, p, tm_in=512, tm_ffn=512):
    bsz, t, _ = x.shape
    n = bsz * t
    assert t % QBLK == 0 and t // QBLK >= KV_PAIRS and t % tm_ffn == 0 and n % tm_in == 0
    bg, gg, q, v, kt, xres = _inproj(x.reshape(n, D_MODEL), p["ln_in_g"], p["ln_in_b"],
                                     p["w_main"], p["w_kt"], tm_in)
    r3 = lambda a: a.reshape(bsz, t, a.shape[-1])
    y = _mixer(r3(bg), r3(gg), r3(q), kt.reshape(bsz, t // QBLK, ATTN_WIDTH, QBLK), r3(v),
               p["bias"], p["conv_w"], p["conv_b"], p["gn_g"])
    x1 = _oproj(y.reshape(n, D_MODEL), xres, p["w_o"], p["ln1_g"], p["ln1_b"], tm_in)
    out = _ffn(x1, t, p["wu"], p["cw"], p["cb"], p["wd"], p["ln2_g"], p["ln2_b"], tm_ffn)
    return out.reshape(bsz, t, D_MODEL)


def _prepare(ln_in_g, ln_in_b, w_in, conv_w, conv_b, rpb, gn_g, w_o, ln1_g, ln1_b,
             w_up, ffn_conv_w, ffn_conv_b, w_down, ln2_g, ln2_b):
    row = lambda a: a.reshape(1, -1).astype(_F32)
    c3 = 3 * CONV_WIDTH
    w_in = w_in[0]
    w_k = w_in[:, c3 + ATTN_WIDTH:c3 + 2 * ATTN_WIDTH]
    w_main = jnp.concatenate([w_in[:, :c3 + ATTN_WIDTH], w_in[:, c3 + 2 * ATTN_WIDTH:]], axis=1)
    w_up = w_up[0]
    fcw = ffn_conv_w[0]
    fcb = ffn_conv_b[0].reshape(1, -1)
    return {
        "ln_in_g": row(ln_in_g), "ln_in_b": row(ln_in_b),
        "w_main": w_main.astype(_BF16), "w_kt": w_k.T.astype(_BF16),
        "conv_w": conv_w[0].astype(_F32), "conv_b": row(conv_b[0]),
        "bias": _attention_bias(rpb[0]), "gn_g": row(gn_g[0]),
        "w_o": w_o[0].astype(_BF16), "ln1_g": row(ln1_g[0]), "ln1_b": row(ln1_b[0]),
        "wu": _gate_val_chunks(w_up).astype(_BF16),
        "cw": _gate_val_chunks(fcw).astype(_F32),
        "cb": _gate_val_chunks(fcb).astype(_F32),
        "wd": w_down[0].reshape(N_FF_CHUNKS, FF_CHUNK, D_MODEL).astype(_BF16),
        "ln2_g": row(ln2_g[0]), "ln2_b": row(ln2_b[0]),
    }


def kernel(x_prompt, x_sample, ln_in_g, ln_in_b, w_in, conv_w, conv_b, rpb, gn_g, w_o, ln1_g,
           ln1_b, w_up, ffn_conv_w, ffn_conv_b, w_down, ln2_g, ln2_b):
    p = _prepare(ln_in_g, ln_in_b, w_in, conv_w, conv_b, rpb, gn_g, w_o, ln1_g, ln1_b,
                 w_up, ffn_conv_w, ffn_conv_b, w_down, ln2_g, ln2_b)
    return (_trunk(x_prompt, p), _trunk(x_sample, p))
```

```python
import functools

import jax
import jax.numpy as jnp
import numpy as np
from jax import lax
from jax.experimental import pallas as pl
from jax.experimental.pallas import tpu as pltpu

D_MODEL = 1024
CONV_WIDTH = 512
ATTN_WIDTH = 512
N_HEADS = 8
HEAD_DIM = 64
GRID_W = 64
NA_ROWS = 8
NA_COLS = 16
D_FF = 2816
LN_EPS = 1e-5
ALPHA = 2.0 ** 0.25
NEG_INF = -1e30

QBLK = 2 * GRID_W
KV_PAIRS = 5
KBLK = KV_PAIRS * QBLK
HALO = 16
FF_CHUNK = 256
ACT_ROWS = 64
LANES = 128
N_GROUPS = 4
OPROJ_GROUPS = 2
MIX_BLOCKS = 2
LOOKAHEAD = 2
N_FF_CHUNKS = D_FF // FF_CHUNK
VMEM_LIMIT_BYTES = 56 * 1024 * 1024

_F32 = jnp.float32
_BF16 = jnp.bfloat16


def _layernorm(x, g, b):
    mu = jnp.mean(x, axis=-1, keepdims=True)
    xc = x - mu
    var = jnp.mean(xc * xc, axis=-1, keepdims=True)
    return xc * lax.rsqrt(var + LN_EPS) * g + b


def _rmsnorm(x, g):
    ms = jnp.mean(x * x, axis=-1, keepdims=True)
    return x * lax.rsqrt(ms + LN_EPS) * g


def _dot(a, b):
    return jnp.dot(a, b, preferred_element_type=_F32)


def _inproj_kernel(x_ref, g_ref, b_ref, w_ref, wkt_ref,
                   bg_ref, gg_ref, q_ref, v_ref, kt_ref, xres_ref):
    tm = x_ref.shape[0]
    xn = _layernorm(x_ref[...], g_ref[...], b_ref[...])
    xres_ref[...] = ALPHA * xn
    xb = xn.astype(_BF16)
    c1, c2, c3 = CONV_WIDTH, 2 * CONV_WIDTH, 3 * CONV_WIDTH
    pa = _dot(xb, w_ref[:, 0:c3])
    bg_ref[...] = pa[:, 0:c1].astype(_BF16)
    gg_ref[...] = (pa[:, c1:c2] * pa[:, c2:c3]).astype(_BF16)
    q = _dot(xb, w_ref[:, c3:c3 + ATTN_WIDTH])
    q_ref[...] = (q * (HEAD_DIM ** -0.5)).astype(_BF16)
    v_ref[...] = _dot(xb, w_ref[:, c3 + ATTN_WIDTH:c3 + 2 * ATTN_WIDTH]).astype(_BF16)
    kt = lax.dot_general(wkt_ref[...], xb, (((1,), (1,)), ((), ())),
                         preferred_element_type=_F32)
    for i in range(tm // QBLK):
        kt_ref[i] = kt[:, i * QBLK:(i + 1) * QBLK].astype(_BF16)


def _inproj(x2d, ln_g, ln_b, w_main, w_kt, tm):
    n = x2d.shape[0]
    const = lambda i: (0, 0)
    tok = lambda i: (i, 0)
    outs = pl.pallas_call(
        _inproj_kernel,
        grid=(n // tm,),
        in_specs=[
            pl.BlockSpec((tm, D_MODEL), tok),
            pl.BlockSpec((1, D_MODEL), const),
            pl.BlockSpec((1, D_MODEL), const),
            pl.BlockSpec(w_main.shape, const),
            pl.BlockSpec(w_kt.shape, const),
        ],
        out_specs=[
            pl.BlockSpec((tm, CONV_WIDTH), tok),
            pl.BlockSpec((tm, CONV_WIDTH), tok),
            pl.BlockSpec((tm, ATTN_WIDTH), tok),
            pl.BlockSpec((tm, ATTN_WIDTH), tok),
            pl.BlockSpec((tm // QBLK, ATTN_WIDTH, QBLK), lambda i: (i, 0, 0)),
            pl.BlockSpec((tm, D_MODEL), tok),
        ],
        out_shape=[
            jax.ShapeDtypeStruct((n, CONV_WIDTH), _BF16),
            jax.ShapeDtypeStruct((n, CONV_WIDTH), _BF16),
            jax.ShapeDtypeStruct((n, ATTN_WIDTH), _BF16),
            jax.ShapeDtypeStruct((n, ATTN_WIDTH), _BF16),
            jax.ShapeDtypeStruct((n // QBLK, ATTN_WIDTH, QBLK), _BF16),
            jax.ShapeDtypeStruct((n, D_MODEL), _F32),
        ],
        compiler_params=pltpu.CompilerParams(
            dimension_semantics=("arbitrary",), vmem_limit_bytes=VMEM_LIMIT_BYTES),
        name="inproj",
    )(x2d, ln_g, ln_b, w_main, w_kt)
    return outs


def _mixer_kernel(nblk, bg_ref, gg_ref, gp_ref, gn_ref, q_ref, kt_ref, v_ref,
                  bias_ref, cw_ref, cb_ref, gng_ref, y_ref):
    step = pl.program_id(1)
    nstep = nblk // MIX_BLOCKS
    tq = MIX_BLOCKS * QBLK
    lane = lax.broadcasted_iota(jnp.int32, (QBLK, 2 * HEAD_DIM), 1)
    low = lane < HEAD_DIM

    def head_cols(h):
        j = h // 2
        return slice(j * 2 * HEAD_DIM, (j + 1) * 2 * HEAD_DIM)

    def block_consts(sub):
        blk = step * MIX_BLOCKS + sub
        p0 = jnp.clip(blk - 2, 0, nblk - KV_PAIRS)
        row0 = pl.multiple_of(p0 * QBLK, QBLK)
        btype = jnp.where(blk == 0, 0, jnp.where(blk == 1, 1, jnp.where(
            blk == nblk - 2, 3, jnp.where(blk == nblk - 1, 4, 2))))
        return p0, row0, btype

    consts = [block_consts(sub) for sub in range(MIX_BLOCKS)]

    def scores(item):
        sub, h = item
        p0, _, btype = consts[sub]
        cs = head_cols(h)
        qp = q_ref[0, sub * QBLK:(sub + 1) * QBLK, cs]
        sel = low if h % 2 == 0 else jnp.logical_not(low)
        qa = jnp.where(sel, qp, jnp.zeros_like(qp))
        ktp = jnp.concatenate([kt_ref[0, p0 + t, cs, :] for t in range(KV_PAIRS)], axis=1)
        return _dot(qa, ktp) + bias_ref[btype, h]

    def softmax_num(s):
        m = jnp.max(s, axis=-1, keepdims=True)
        p = jnp.exp(s - m)
        return p.astype(_BF16), jnp.sum(p, axis=-1, keepdims=True)

    def weighted(item, p, l):
        sub, h = item
        return _dot(p, v_ref[0, pl.ds(consts[sub][1], KBLK), head_cols(h)]) / l

    def attention():
        items = [(sub, h) for h in range(N_HEADS) for sub in range(MIX_BLOCKS)]
        s = {k: scores(items[k]) for k in range(LOOKAHEAD)}
        outs = {}
        for k, item in enumerate(items):
            p, l = softmax_num(s.pop(k))
            if k + LOOKAHEAD < len(items):
                s[k + LOOKAHEAD] = scores(items[k + LOOKAHEAD])
            outs[item] = weighted(item, p, l)
        for sub in range(MIX_BLOCKS):
            y_attn = jnp.concatenate(
                [jnp.where(low, outs[(sub, 2 * j)], outs[(sub, 2 * j + 1)])
                 for j in range(N_HEADS // 2)], axis=1)
            y_ref[0, sub * QBLK:(sub + 1) * QBLK, CONV_WIDTH:] = _rmsnorm(
                y_attn, gng_ref[:, CONV_WIDTH:]).astype(_BF16)

    def conv_mixer():
        g = gg_ref[0].astype(_F32)
        prev_row = gp_ref[0].astype(_F32)[HALO - 1:HALO, :]
        next_row = gn_ref[0].astype(_F32)[0:1, :]
        prev_row = jnp.where(step > 0, prev_row, jnp.zeros_like(prev_row))
        next_row = jnp.where(step < nstep - 1, next_row, jnp.zeros_like(next_row))
        row = lax.broadcasted_iota(jnp.int32, g.shape, 0)
        g_m1 = jnp.where(row == 0, prev_row, pltpu.roll(g, 1, 0))
        g_p1 = jnp.where(row == tq - 1, next_row, pltpu.roll(g, tq - 1, 0))
        cw = cw_ref[...]
        conv = cw[0:1] * g_m1 + cw[1:2] * g + cw[2:3] * g_p1 + cb_ref[...]
        y_conv = bg_ref[0].astype(_F32) * conv
        y_ref[0, :, 0:CONV_WIDTH] = _rmsnorm(y_conv, gng_ref[:, :CONV_WIDTH]).astype(_BF16)

    conv_mixer()
    attention()


def _mixer(bg, gg, q, kt, v, bias, cw, cb, gng):
    bsz, t, _ = q.shape
    nblk = t // QBLK
    tq = MIX_BLOCKS * QBLK
    nhalo = t // HALO
    per = tq // HALO
    tok = lambda b, i: (b, i, 0)
    c2 = lambda b, i: (0, 0)
    return pl.pallas_call(
        functools.partial(_mixer_kernel, nblk),
        grid=(bsz, nblk // MIX_BLOCKS),
        in_specs=[
            pl.BlockSpec((1, tq, CONV_WIDTH), tok),
            pl.BlockSpec((1, tq, CONV_WIDTH), tok),
            pl.BlockSpec((1, HALO, CONV_WIDTH), lambda b, i: (b, jnp.maximum(i * per - 1, 0), 0)),
            pl.BlockSpec((1, HALO, CONV_WIDTH),
                         lambda b, i: (b, jnp.minimum((i + 1) * per, nhalo - 1), 0)),
            pl.BlockSpec((1, tq, ATTN_WIDTH), tok),
            pl.BlockSpec((1, nblk, ATTN_WIDTH, QBLK), lambda b, i: (b, 0, 0, 0)),
            pl.BlockSpec((1, t, ATTN_WIDTH), lambda b, i: (b, 0, 0)),
            pl.BlockSpec(bias.shape, lambda b, i: (0, 0, 0, 0), pipeline_mode=pl.Buffered(1)),
            pl.BlockSpec((3, CONV_WIDTH), c2),
            pl.BlockSpec((1, CONV_WIDTH), c2),
            pl.BlockSpec((1, D_MODEL), c2),
        ],
        out_specs=pl.BlockSpec((1, tq, D_MODEL), tok),
        out_shape=jax.ShapeDtypeStruct((bsz, t, D_MODEL), _BF16),
        compiler_params=pltpu.CompilerParams(
            dimension_semantics=("arbitrary", "arbitrary"), vmem_limit_bytes=VMEM_LIMIT_BYTES),
        name="mixer",
    )(bg, gg, gg, gg, q, kt, v, bias, cw, cb, gng)


def _oproj_kernel(y_ref, xres_ref, wo_ref, g_ref, b_ref, o_ref):
    tm = y_ref.shape[0]
    rows = tm // OPROJ_GROUPS
    for r in range(0, tm, rows):
        z = xres_ref[r:r + rows, :] + _dot(y_ref[r:r + rows, :], wo_ref[...])
        o_ref[r:r + rows, :] = _layernorm(z, g_ref[...], b_ref[...])


def _oproj(y2d, xres, w_o, l1g, l1b, tm):
    n = y2d.shape[0]
    tok = lambda i: (i, 0)
    c2 = lambda i: (0, 0)
    return pl.pallas_call(
        _oproj_kernel,
        grid=(n // tm,),
        in_specs=[
            pl.BlockSpec((tm, D_MODEL), tok),
            pl.BlockSpec((tm, D_MODEL), tok),
            pl.BlockSpec((D_MODEL, D_MODEL), c2),
            pl.BlockSpec((1, D_MODEL), c2),
            pl.BlockSpec((1, D_MODEL), c2),
        ],
        out_specs=pl.BlockSpec((tm, D_MODEL), tok),
        out_shape=jax.ShapeDtypeStruct((n, D_MODEL), _F32),
        compiler_params=pltpu.CompilerParams(
            dimension_semantics=("arbitrary",), vmem_limit_bytes=VMEM_LIMIT_BYTES),
        name="oproj",
    )(y2d, xres, w_o, l1g, l1b)


def _attention_bias(rpb):
    rows = 4 * KV_PAIRS
    nblk = rows // 2
    cols = np.arange(GRID_W)
    cstart = np.clip(cols - NA_COLS // 2, 0, GRID_W - NA_COLS)
    col_ok = (cols[None, :] >= cstart[:, None]) & (cols[None, :] < cstart[:, None] + NA_COLS)
    pad = GRID_W - NA_COLS
    rp = jnp.pad(rpb.astype(_F32), ((0, 0), (0, 0), (pad, pad)))
    toep = jnp.stack([rp[:, :, GRID_W - 1 - c:2 * GRID_W - 1 - c] for c in range(GRID_W)], axis=2)
    bfull = jnp.where(col_ok[None, None], toep, NEG_INF)
    masked = jnp.full((N_HEADS, GRID_W, GRID_W), NEG_INF, _F32)
    out = []
    for i in (0, 1, 2, nblk - 2, nblk - 1):
        p0 = min(max(i - 2, 0), nblk - KV_PAIRS)
        blocks = []
        for r in (2 * i, 2 * i + 1):
            rs = min(max(r - NA_ROWS // 2, 0), rows - NA_ROWS)
            tiles = []
            for kr in range(2 * p0, 2 * p0 + 2 * KV_PAIRS):
                inside = rs <= kr < rs + NA_ROWS
                tiles.append(bfull[:, kr - r + NA_ROWS - 1] if inside else masked)
            blocks.append(jnp.concatenate(tiles, axis=2))
        out.append(jnp.concatenate(blocks, axis=1))
    return jnp.stack(out)


def _ffn_kernel(tiles_per_seq, x_ref, xp_ref, xn_ref, wu_ref, cw_ref, cb_ref, wd_ref,
                l2g_ref, l2b_ref, o_ref, xb_scr, h0_scr, h1_scr, a0_scr, a1_scr, acc_scr):
    tm = x_ref.shape[0]
    h_scr = (h0_scr, h1_scr)
    a_scr = (a0_scr, a1_scr)
    i = pl.program_id(0)
    first = (i % tiles_per_seq) == 0
    last = (i % tiles_per_seq) == tiles_per_seq - 1
    xt = x_ref[...]
    xp = xp_ref[...]
    xnx = xn_ref[...]
    xb_scr[0:HALO] = jnp.where(first, jnp.zeros_like(xp), xp).astype(_BF16)
    xb_scr[HALO:HALO + tm] = xt.astype(_BF16)
    xb_scr[HALO + tm:] = jnp.where(last, jnp.zeros_like(xnx), xnx).astype(_BF16)
    acc_scr[...] = ALPHA * xt

    rows_g = tm // N_GROUPS

    def up(c, slot, g):
        lo = 0 if g == 0 else HALO + g * rows_g
        hi = HALO + (g + 1) * rows_g + (HALO if g == N_GROUPS - 1 else 0)
        h_scr[slot][lo:hi, :] = _dot(xb_scr[lo:hi, :], wu_ref[c])

    def act(c, slot, g):
        w = cw_ref[c]
        b = cb_ref[c]
        pad = 8
        for r in range(g * rows_g, (g + 1) * rows_g, ACT_ROWS):
            for j in range(0, FF_CHUNK, LANES):
                def conv(col):
                    cs = slice(col, col + LANES)
                    hs = h_scr[slot][pl.ds(HALO + r - pad, ACT_ROWS + 2 * pad), cs]
                    h_m1 = pltpu.roll(hs, 1, 0)[pad:pad + ACT_ROWS]
                    h_p1 = pltpu.roll(hs, ACT_ROWS + 2 * pad - 1, 0)[pad:pad + ACT_ROWS]
                    return (w[0:1, cs] * h_m1 + w[1:2, cs] * hs[pad:pad + ACT_ROWS]
                            + w[2:3, cs] * h_p1 + b[:, cs])
                a = jax.nn.gelu(conv(j)) * conv(FF_CHUNK + j)
                a_scr[slot][r:r + ACT_ROWS, j:j + LANES] = a.astype(_BF16)

    def down(c, slot, g):
        rs = slice(g * rows_g, (g + 1) * rows_g)
        acc_scr[rs, :] += _dot(a_scr[slot][rs, :], wd_ref[c])

    def step(c, do_up=True, do_act=True, do_down=True):
        slot = c % 2
        for g in range(N_GROUPS):
            if do_up:
                up(c + 1, 1 - slot, g)
            if do_act:
                act(c, slot, g)
            if do_down:
                down(c - 1, 1 - slot, g)

    step(-1, do_act=False, do_down=False)
    step(0, do_down=False)
    for c in range(1, N_FF_CHUNKS - 1):
        step(c)
    step(N_FF_CHUNKS - 1, do_up=False)
    step(N_FF_CHUNKS, do_up=False, do_act=False)
    o_ref[...] = _layernorm(acc_scr[...], l2g_ref[...], l2b_ref[...])


def _ffn(x2d, t, wu, cw, cb, wd, l2g, l2b, tm):
    n = x2d.shape[0]
    per = tm // HALO
    nhalo = n // HALO
    tok = lambda i: (i, 0)
    c2 = lambda i: (0, 0)
    c3 = lambda i: (0, 0, 0)
    return pl.pallas_call(
        functools.partial(_ffn_kernel, t // tm),
        grid=(n // tm,),
        in_specs=[
            pl.BlockSpec((tm, D_MODEL), tok),
            pl.BlockSpec((HALO, D_MODEL), lambda i: (jnp.maximum(i * per - 1, 0), 0)),
            pl.BlockSpec((HALO, D_MODEL), lambda i: (jnp.minimum((i + 1) * per, nhalo - 1), 0)),
            pl.BlockSpec(wu.shape, c3),
            pl.BlockSpec(cw.shape, c3),
            pl.BlockSpec(cb.shape, c3),
            pl.BlockSpec(wd.shape, c3),
            pl.BlockSpec((1, D_MODEL), c2),
            pl.BlockSpec((1, D_MODEL), c2),
        ],
        out_specs=pl.BlockSpec((tm, D_MODEL), tok),
        out_shape=jax.ShapeDtypeStruct((n, D_MODEL), _F32),
        scratch_shapes=[
            pltpu.VMEM((tm + 2 * HALO, D_MODEL), _BF16),
            pltpu.VMEM((tm + 2 * HALO, 2 * FF_CHUNK), _F32),
            pltpu.VMEM((tm + 2 * HALO, 2 * FF_CHUNK), _F32),
            pltpu.VMEM((tm, FF_CHUNK), _BF16),
            pltpu.VMEM((tm, FF_CHUNK), _BF16),
            pltpu.VMEM((tm, D_MODEL), _F32),
        ],
        compiler_params=pltpu.CompilerParams(
            dimension_semantics=("arbitrary",), vmem_limit_bytes=VMEM_LIMIT_BYTES),
        name="convffn",
    )(x2d, x2d, x2d, wu, cw, cb, wd, l2g, l2b)


def _gate_val_chunks(w):
    r = w.shape[0]
    w = w.reshape(r, 2, N_FF_CHUNKS, FF_CHUNK).transpose(2, 0, 1, 3)
    return w.reshape(N_FF_CHUNKS, r, 2 * FF_CHUNK)


def _trunk(x, p, tm_in=512, tm_ffn=512):
    bsz, t, _ = x.shape
    n = bsz * t
    assert t % QBLK == 0 and t // QBLK >= KV_PAIRS and t % tm_ffn == 0 and n % tm_in == 0
    bg, gg, q, v, kt, xres = _inproj(x.reshape(n, D_MODEL), p["ln_in_g"], p["ln_in_b"],
                                     p["w_main"], p["w_kt"], tm_in)
    r3 = lambda a: a.reshape(bsz, t, a.shape[-1])
    y = _mixer(r3(bg), r3(gg), r3(q), kt.reshape(bsz, t // QBLK, ATTN_WIDTH, QBLK), r3(v),
               p["bias"], p["conv_w"], p["conv_b"], p["gn_g"])
    x1 = _oproj(y.reshape(n, D_MODEL), xres, p["w_o"], p["ln1_g"], p["ln1_b"], tm_in)
    out = _ffn(x1, t, p["wu"], p["cw"], p["cb"], p["wd"], p["ln2_g"], p["ln2_b"], tm_ffn)
    return out.reshape(bsz, t, D_MODEL)


def _prepare(ln_in_g, ln_in_b, w_in, conv_w, conv_b, rpb, gn_g, w_o, ln1_g, ln1_b,
             w_up, ffn_conv_w, ffn_conv_b, w_down, ln2_g, ln2_b):
    row = lambda a: a.reshape(1, -1).astype(_F32)
    c3 = 3 * CONV_WIDTH
    w_in = w_in[0]
    w_k = w_in[:, c3 + ATTN_WIDTH:c3 + 2 * ATTN_WIDTH]
    w_main = jnp.concatenate([w_in[:, :c3 + ATTN_WIDTH], w_in[:, c3 + 2 * ATTN_WIDTH:]], axis=1)
    w_up = w_up[0]
    fcw = ffn_conv_w[0]
    fcb = ffn_conv_b[0].reshape(1, -1)
    return {
        "ln_in_g": row(ln_in_g), "ln_in_b": row(ln_in_b),
        "w_main": w_main.astype(_BF16), "w_kt": w_k.T.astype(_BF16),
        "conv_w": conv_w[0].astype(_F32), "conv_b": row(conv_b[0]),
        "bias": _attention_bias(rpb[0]), "gn_g": row(gn_g[0]),
        "w_o": w_o[0].astype(_BF16), "ln1_g": row(ln1_g[0]), "ln1_b": row(ln1_b[0]),
        "wu": _gate_val_chunks(w_up).astype(_BF16),
        "cw": _gate_val_chunks(fcw).astype(_F32),
        "cb": _gate_val_chunks(fcb).astype(_F32),
        "wd": w_down[0].reshape(N_FF_CHUNKS, FF_CHUNK, D_MODEL).astype(_BF16),
        "ln2_g": row(ln2_g[0]), "ln2_b": row(ln2_b[0]),
    }


def kernel(x_prompt, x_sample, ln_in_g, ln_in_b, w_in, conv_w, conv_b, rpb, gn_g, w_o, ln1_g,
           ln1_b, w_up, ffn_conv_w, ffn_conv_b, w_down, ln2_g, ln2_b):
    p = _prepare(ln_in_g, ln_in_b, w_in, conv_w, conv_b, rpb, gn_g, w_o, ln1_g, ln1_b,
                 w_up, ffn_conv_w, ffn_conv_b, w_down, ln2_g, ln2_b)
    return (_trunk(x_prompt, p), _trunk(x_sample, p))
```

```python
import functools

import jax
import jax.numpy as jnp
import numpy as np
from jax import lax
from jax.experimental import pallas as pl
from jax.experimental.pallas import tpu as pltpu

D_MODEL = 1024
CONV_WIDTH = 512
ATTN_WIDTH = 512
N_HEADS = 8
HEAD_DIM = 64
GRID_W = 64
NA_ROWS = 8
NA_COLS = 16
D_FF = 2816
LN_EPS = 1e-5
ALPHA = 2.0 ** 0.25
NEG_INF = -1e30
LOG2E = 1.4426950408889634

QBLK = 2 * GRID_W
KV_PAIRS = 5
KBLK = KV_PAIRS * QBLK
HALO = 16
FF_CHUNK = 256
ACT_ROWS = 64
LANES = 128
N_GROUPS = 4
OPROJ_GROUPS = 2
MIX_BLOCKS = 2
LOOKAHEAD = 2
N_FF_CHUNKS = D_FF // FF_CHUNK
VMEM_LIMIT_BYTES = 56 * 1024 * 1024

_F32 = jnp.float32
_BF16 = jnp.bfloat16


def _layernorm(x, g, b):
    mu = jnp.mean(x, axis=-1, keepdims=True)
    xc = x - mu
    var = jnp.mean(xc * xc, axis=-1, keepdims=True)
    return xc * lax.rsqrt(var + LN_EPS) * g + b


def _rmsnorm(x, g):
    ms = jnp.mean(x * x, axis=-1, keepdims=True)
    return x * lax.rsqrt(ms + LN_EPS) * g


def _dot(a, b):
    return jnp.dot(a, b, preferred_element_type=_F32)


def _inproj_kernel(x_ref, g_ref, b_ref, w_ref, wkt_ref,
                   bg_ref, gg_ref, q_ref, v_ref, kt_ref, xres_ref):
    tm = x_ref.shape[0]
    xn = _layernorm(x_ref[...], g_ref[...], b_ref[...])
    xres_ref[...] = ALPHA * xn
    xb = xn.astype(_BF16)
    c1, c2, c3 = CONV_WIDTH, 2 * CONV_WIDTH, 3 * CONV_WIDTH
    pa = _dot(xb, w_ref[:, 0:c3])
    bg_ref[...] = pa[:, 0:c1].astype(_BF16)
    gg_ref[...] = (pa[:, c1:c2] * pa[:, c2:c3]).astype(_BF16)
    q = _dot(xb, w_ref[:, c3:c3 + ATTN_WIDTH])
    q_ref[...] = (q * (HEAD_DIM ** -0.5 * LOG2E)).astype(_BF16)
    v_ref[...] = _dot(xb, w_ref[:, c3 + ATTN_WIDTH:c3 + 2 * ATTN_WIDTH]).astype(_BF16)
    kt = lax.dot_general(wkt_ref[...], xb, (((1,), (1,)), ((), ())),
                         preferred_element_type=_F32)
    for i in range(tm // QBLK):
        kt_ref[i] = kt[:, i * QBLK:(i + 1) * QBLK].astype(_BF16)


def _inproj(x2d, ln_g, ln_b, w_main, w_kt, tm):
    n = x2d.shape[0]
    const = lambda i: (0, 0)
    tok = lambda i: (i, 0)
    outs = pl.pallas_call(
        _inproj_kernel,
        grid=(n // tm,),
        in_specs=[
            pl.BlockSpec((tm, D_MODEL), tok),
            pl.BlockSpec((1, D_MODEL), const),
            pl.BlockSpec((1, D_MODEL), const),
            pl.BlockSpec(w_main.shape, const),
            pl.BlockSpec(w_kt.shape, const),
        ],
        out_specs=[
            pl.BlockSpec((tm, CONV_WIDTH), tok),
            pl.BlockSpec((tm, CONV_WIDTH), tok),
            pl.BlockSpec((tm, ATTN_WIDTH), tok),
            pl.BlockSpec((tm, ATTN_WIDTH), tok),
            pl.BlockSpec((tm // QBLK, ATTN_WIDTH, QBLK), lambda i: (i, 0, 0)),
            pl.BlockSpec((tm, D_MODEL), tok),
        ],
        out_shape=[
            jax.ShapeDtypeStruct((n, CONV_WIDTH), _BF16),
            jax.ShapeDtypeStruct((n, CONV_WIDTH), _BF16),
            jax.ShapeDtypeStruct((n, ATTN_WIDTH), _BF16),
            jax.ShapeDtypeStruct((n, ATTN_WIDTH), _BF16),
            jax.ShapeDtypeStruct((n // QBLK, ATTN_WIDTH, QBLK), _BF16),
            jax.ShapeDtypeStruct((n, D_MODEL), _F32),
        ],
        compiler_params=pltpu.CompilerParams(
            dimension_semantics=("arbitrary",), vmem_limit_bytes=VMEM_LIMIT_BYTES),
        name="inproj",
    )(x2d, ln_g, ln_b, w_main, w_kt)
    return outs


def _mixer_kernel(nblk, bg_ref, gg_ref, gp_ref, gn_ref, q_ref, kt_ref, v_ref,
                  bias_ref, cw_ref, cb_ref, gng_ref, y_ref):
    step = pl.program_id(1)
    nstep = nblk // MIX_BLOCKS
    tq = MIX_BLOCKS * QBLK
    lane = lax.broadcasted_iota(jnp.int32, (QBLK, 2 * HEAD_DIM), 1)
    low = lane < HEAD_DIM

    def head_cols(h):
        j = h // 2
        return slice(j * 2 * HEAD_DIM, (j + 1) * 2 * HEAD_DIM)

    def block_consts(sub):
        blk = step * MIX_BLOCKS + sub
        p0 = jnp.clip(blk - 2, 0, nblk - KV_PAIRS)
        row0 = pl.multiple_of(p0 * QBLK, QBLK)
        btype = jnp.where(blk == 0, 0, jnp.where(blk == 1, 1, jnp.where(
            blk == nblk - 2, 3, jnp.where(blk == nblk - 1, 4, 2))))
        return p0, row0, btype

    consts = [block_consts(sub) for sub in range(MIX_BLOCKS)]

    def scores(item):
        sub, h = item
        p0, _, btype = consts[sub]
        cs = head_cols(h)
        qp = q_ref[0, sub * QBLK:(sub + 1) * QBLK, cs]
        sel = low if h % 2 == 0 else jnp.logical_not(low)
        qa = jnp.where(sel, qp, jnp.zeros_like(qp))
        ktp = jnp.concatenate([kt_ref[0, p0 + t, cs, :] for t in range(KV_PAIRS)], axis=1)
        return _dot(qa, ktp) + bias_ref[btype, h]

    def softmax_num(s):
        m = jnp.max(s, axis=-1, keepdims=True)
        p = jnp.exp2(s - m)
        return p.astype(_BF16), jnp.sum(p, axis=-1, keepdims=True)

    def weighted(item, p, l):
        sub, h = item
        return _dot(p, v_ref[0, pl.ds(consts[sub][1], KBLK), head_cols(h)]) / l

    def attention():
        items = [(sub, h) for h in range(N_HEADS) for sub in range(MIX_BLOCKS)]
        s = {k: scores(items[k]) for k in range(LOOKAHEAD)}
        outs = {}
        for k, item in enumerate(items):
            p, l = softmax_num(s.pop(k))
            if k + LOOKAHEAD < len(items):
                s[k + LOOKAHEAD] = scores(items[k + LOOKAHEAD])
            outs[item] = weighted(item, p, l)
        for sub in range(MIX_BLOCKS):
            y_attn = jnp.concatenate(
                [jnp.where(low, outs[(sub, 2 * j)], outs[(sub, 2 * j + 1)])
                 for j in range(N_HEADS // 2)], axis=1)
            y_ref[0, sub * QBLK:(sub + 1) * QBLK, CONV_WIDTH:] = _rmsnorm(
                y_attn, gng_ref[:, CONV_WIDTH:]).astype(_BF16)

    def conv_mixer():
        g = gg_ref[0].astype(_F32)
        prev_row = gp_ref[0].astype(_F32)[HALO - 1:HALO, :]
        next_row = gn_ref[0].astype(_F32)[0:1, :]
        prev_row = jnp.where(step > 0, prev_row, jnp.zeros_like(prev_row))
        next_row = jnp.where(step < nstep - 1, next_row, jnp.zeros_like(next_row))
        row = lax.broadcasted_iota(jnp.int32, g.shape, 0)
        g_m1 = jnp.where(row == 0, prev_row, pltpu.roll(g, 1, 0))
        g_p1 = jnp.where(row == tq - 1, next_row, pltpu.roll(g, tq - 1, 0))
        cw = cw_ref[...]
        conv = cw[0:1] * g_m1 + cw[1:2] * g + cw[2:3] * g_p1 + cb_ref[...]
        y_conv = bg_ref[0].astype(_F32) * conv
        y_ref[0, :, 0:CONV_WIDTH] = _rmsnorm(y_conv, gng_ref[:, :CONV_WIDTH]).astype(_BF16)

    conv_mixer()
    attention()


def _mixer(bg, gg, q, kt, v, bias, cw, cb, gng):
    bsz, t, _ = q.shape
    nblk = t // QBLK
    tq = MIX_BLOCKS * QBLK
    nhalo = t // HALO
    per = tq // HALO
    tok = lambda b, i: (b, i, 0)
    c2 = lambda b, i: (0, 0)
    return pl.pallas_call(
        functools.partial(_mixer_kernel, nblk),
        grid=(bsz, nblk // MIX_BLOCKS),
        in_specs=[
            pl.BlockSpec((1, tq, CONV_WIDTH), tok),
            pl.BlockSpec((1, tq, CONV_WIDTH), tok),
            pl.BlockSpec((1, HALO, CONV_WIDTH), lambda b, i: (b, jnp.maximum(i * per - 1, 0), 0)),
            pl.BlockSpec((1, HALO, CONV_WIDTH),
                         lambda b, i: (b, jnp.minimum((i + 1) * per, nhalo - 1), 0)),
            pl.BlockSpec((1, tq, ATTN_WIDTH), tok),
            pl.BlockSpec((1, nblk, ATTN_WIDTH, QBLK), lambda b, i: (b, 0, 0, 0)),
            pl.BlockSpec((1, t, ATTN_WIDTH), lambda b, i: (b, 0, 0)),
            pl.BlockSpec(bias.shape, lambda b, i: (0, 0, 0, 0), pipeline_mode=pl.Buffered(1)),
            pl.BlockSpec((3, CONV_WIDTH), c2),
            pl.BlockSpec((1, CONV_WIDTH), c2),
            pl.BlockSpec((1, D_MODEL), c2),
        ],
        out_specs=pl.BlockSpec((1, tq, D_MODEL), tok),
        out_shape=jax.ShapeDtypeStruct((bsz, t, D_MODEL), _BF16),
        compiler_params=pltpu.CompilerParams(
            dimension_semantics=("arbitrary", "arbitrary"), vmem_limit_bytes=VMEM_LIMIT_BYTES),
        name="mixer",
    )(bg, gg, gg, gg, q, kt, v, bias, cw, cb, gng)


def _oproj_kernel(y_ref, xres_ref, wo_ref, g_ref, b_ref, o_ref):
    tm = y_ref.shape[0]
    rows = tm // OPROJ_GROUPS
    for r in range(0, tm, rows):
        z = xres_ref[r:r + rows, :] + _dot(y_ref[r:r + rows, :], wo_ref[...])
        o_ref[r:r + rows, :] = _layernorm(z, g_ref[...], b_ref[...])


def _oproj(y2d, xres, w_o, l1g, l1b, tm):
    n = y2d.shape[0]
    tok = lambda i: (i, 0)
    c2 = lambda i: (0, 0)
    return pl.pallas_call(
        _oproj_kernel,
        grid=(n // tm,),
        in_specs=[
            pl.BlockSpec((tm, D_MODEL), tok),
            pl.BlockSpec((tm, D_MODEL), tok),
            pl.BlockSpec((D_MODEL, D_MODEL), c2),
            pl.BlockSpec((1, D_MODEL), c2),
            pl.BlockSpec((1, D_MODEL), c2),
        ],
        out_specs=pl.BlockSpec((tm, D_MODEL), tok),
        out_shape=jax.ShapeDtypeStruct((n, D_MODEL), _F32),
        compiler_params=pltpu.CompilerParams(
            dimension_semantics=("arbitrary",), vmem_limit_bytes=VMEM_LIMIT_BYTES),
        name="oproj",
    )(y2d, xres, w_o, l1g, l1b)


def _attention_bias(rpb):
    rows = 4 * KV_PAIRS
    nblk = rows // 2
    cols = np.arange(GRID_W)
    cstart = np.clip(cols - NA_COLS // 2, 0, GRID_W - NA_COLS)
    col_ok = (cols[None, :] >= cstart[:, None]) & (cols[None, :] < cstart[:, None] + NA_COLS)
    pad = GRID_W - NA_COLS
    rp = jnp.pad(rpb.astype(_F32), ((0, 0), (0, 0), (pad, pad)))
    toep = jnp.stack([rp[:, :, GRID_W - 1 - c:2 * GRID_W - 1 - c] for c in range(GRID_W)], axis=2)
    bfull = jnp.where(col_ok[None, None], toep, NEG_INF)
    masked = jnp.full((N_HEADS, GRID_W, GRID_W), NEG_INF, _F32)
    out = []
    for i in (0, 1, 2, nblk - 2, nblk - 1):
        p0 = min(max(i - 2, 0), nblk - KV_PAIRS)
        blocks = []
        for r in (2 * i, 2 * i + 1):
            rs = min(max(r - NA_ROWS // 2, 0), rows - NA_ROWS)
            tiles = []
            for kr in range(2 * p0, 2 * p0 + 2 * KV_PAIRS):
                inside = rs <= kr < rs + NA_ROWS
                tiles.append(bfull[:, kr - r + NA_ROWS - 1] if inside else masked)
            blocks.append(jnp.concatenate(tiles, axis=2))
        out.append(jnp.concatenate(blocks, axis=1))
    return jnp.stack(out)


def _ffn_kernel(tiles_per_seq, x_ref, xp_ref, xn_ref, wu_ref, cw_ref, cb_ref, wd_ref,
                l2g_ref, l2b_ref, o_ref, xb_scr, h0_scr, h1_scr, a0_scr, a1_scr, acc_scr):
    tm = x_ref.shape[0]
    h_scr = (h0_scr, h1_scr)
    a_scr = (a0_scr, a1_scr)
    i = pl.program_id(0)
    first = (i % tiles_per_seq) == 0
    last = (i % tiles_per_seq) == tiles_per_seq - 1
    xt = x_ref[...]
    xp = xp_ref[...]
    xnx = xn_ref[...]
    xb_scr[0:HALO] = jnp.where(first, jnp.zeros_like(xp), xp).astype(_BF16)
    xb_scr[HALO:HALO + tm] = xt.astype(_BF16)
    xb_scr[HALO + tm:] = jnp.where(last, jnp.zeros_like(xnx), xnx).astype(_BF16)
    acc_scr[...] = ALPHA * xt

    rows_g = tm // N_GROUPS

    def up(c, slot, g):
        lo = 0 if g == 0 else HALO + g * rows_g
        hi = HALO + (g + 1) * rows_g + (HALO if g == N_GROUPS - 1 else 0)
        h_scr[slot][lo:hi, :] = _dot(xb_scr[lo:hi, :], wu_ref[c])

    def act(c, slot, g):
        w = cw_ref[c]
        b = cb_ref[c]
        pad = 8
        for r in range(g * rows_g, (g + 1) * rows_g, ACT_ROWS):
            for j in range(0, FF_CHUNK, LANES):
                def conv(col):
                    cs = slice(col, col + LANES)
                    hs = h_scr[slot][pl.ds(HALO + r - pad, ACT_ROWS + 2 * pad), cs]
                    h_m1 = pltpu.roll(hs, 1, 0)[pad:pad + ACT_ROWS]
                    h_p1 = pltpu.roll(hs, ACT_ROWS + 2 * pad - 1, 0)[pad:pad + ACT_ROWS]
                    return (w[0:1, cs] * h_m1 + w[1:2, cs] * hs[pad:pad + ACT_ROWS]
                            + w[2:3, cs] * h_p1 + b[:, cs])
                a = jax.nn.gelu(conv(j)) * conv(FF_CHUNK + j)
                a_scr[slot][r:r + ACT_ROWS, j:j + LANES] = a.astype(_BF16)

    def down(c, slot, g):
        rs = slice(g * rows_g, (g + 1) * rows_g)
        acc_scr[rs, :] += _dot(a_scr[slot][rs, :], wd_ref[c])

    def step(c, do_up=True, do_act=True, do_down=True):
        slot = c % 2
        for g in range(N_GROUPS):
            if do_up:
                up(c + 1, 1 - slot, g)
            if do_act:
                act(c, slot, g)
            if do_down:
                down(c - 1, 1 - slot, g)

    step(-1, do_act=False, do_down=False)
    step(0, do_down=False)
    for c in range(1, N_FF_CHUNKS - 1):
        step(c)
    step(N_FF_CHUNKS - 1, do_up=False)
    step(N_FF_CHUNKS, do_up=False, do_act=False)
    o_ref[...] = _layernorm(acc_scr[...], l2g_ref[...], l2b_ref[...])


def _ffn(x2d, t, wu, cw, cb, wd, l2g, l2b, tm):
    n = x2d.shape[0]
    per = tm // HALO
    nhalo = n // HALO
    tok = lambda i: (i, 0)
    c2 = lambda i: (0, 0)
    c3 = lambda i: (0, 0, 0)
    return pl.pallas_call(
        functools.partial(_ffn_kernel, t // tm),
        grid=(n // tm,),
        in_specs=[
            pl.BlockSpec((tm, D_MODEL), tok),
            pl.BlockSpec((HALO, D_MODEL), lambda i: (jnp.maximum(i * per - 1, 0), 0)),
            pl.BlockSpec((HALO, D_MODEL), lambda i: (jnp.minimum((i + 1) * per, nhalo - 1), 0)),
            pl.BlockSpec(wu.shape, c3),
            pl.BlockSpec(cw.shape, c3),
            pl.BlockSpec(cb.shape, c3),
            pl.BlockSpec(wd.shape, c3),
            pl.BlockSpec((1, D_MODEL), c2),
            pl.BlockSpec((1, D_MODEL), c2),
        ],
        out_specs=pl.BlockSpec((tm, D_MODEL), tok),
        out_shape=jax.ShapeDtypeStruct((n, D_MODEL), _F32),
        scratch_shapes=[
            pltpu.VMEM((tm + 2 * HALO, D_MODEL), _BF16),
            pltpu.VMEM((tm + 2 * HALO, 2 * FF_CHUNK), _F32),
            pltpu.VMEM((tm + 2 * HALO, 2 * FF_CHUNK), _F32),
            pltpu.VMEM((tm, FF_CHUNK), _BF16),
            pltpu.VMEM((tm, FF_CHUNK), _BF16),
            pltpu.VMEM((tm, D_MODEL), _F32),
        ],
        compiler_params=pltpu.CompilerParams(
            dimension_semantics=("arbitrary",), vmem_limit_bytes=VMEM_LIMIT_BYTES),
        name="convffn",
    )(x2d, x2d, x2d, wu, cw, cb, wd, l2g, l2b)


def _gate_val_chunks(w):
    r = w.shape[0]
    w = w.reshape(r, 2, N_FF_CHUNKS, FF_CHUNK).transpose(2, 0, 1, 3)
    return w.reshape(N_FF_CHUNKS, r, 2 * FF_CHUNK)


def _trunk(x, p, tm_in=512, tm_ffn=512):
    bsz, t, _ = x.shape
    n = bsz * t
    assert t % QBLK == 0 and t // QBLK >= KV_PAIRS and t % tm_ffn == 0 and n % tm_in == 0
    bg, gg, q, v, kt, xres = _inproj(x.reshape(n, D_MODEL), p["ln_in_g"], p["ln_in_b"],
                                     p["w_main"], p["w_kt"], tm_in)
    r3 = lambda a: a.reshape(bsz, t, a.shape[-1])
    y = _mixer(r3(bg), r3(gg), r3(q), kt.reshape(bsz, t // QBLK, ATTN_WIDTH, QBLK), r3(v),
               p["bias"], p["conv_w"], p["conv_b"], p["gn_g"])
    x1 = _oproj(y.reshape(n, D_MODEL), xres, p["w_o"], p["ln1_g"], p["ln1_b"], tm_in)
    out = _ffn(x1, t, p["wu"], p["cw"], p["cb"], p["wd"], p["ln2_g"], p["ln2_b"], tm_ffn)
    return out.reshape(bsz, t, D_MODEL)


def _prepare(ln_in_g, ln_in_b, w_in, conv_w, conv_b, rpb, gn_g, w_o, ln1_g, ln1_b,
             w_up, ffn_conv_w, ffn_conv_b, w_down, ln2_g, ln2_b):
    row = lambda a: a.reshape(1, -1).astype(_F32)
    c3 = 3 * CONV_WIDTH
    w_in = w_in[0]
    w_k = w_in[:, c3 + ATTN_WIDTH:c3 + 2 * ATTN_WIDTH]
    w_main = jnp.concatenate([w_in[:, :c3 + ATTN_WIDTH], w_in[:, c3 + 2 * ATTN_WIDTH:]], axis=1)
    w_up = w_up[0]
    fcw = ffn_conv_w[0]
    fcb = ffn_conv_b[0].reshape(1, -1)
    return {
        "ln_in_g": row(ln_in_g), "ln_in_b": row(ln_in_b),
        "w_main": w_main.astype(_BF16), "w_kt": w_k.T.astype(_BF16),
        "conv_w": conv_w[0].astype(_F32), "conv_b": row(conv_b[0]),
        "bias": _attention_bias(rpb[0] * LOG2E), "gn_g": row(gn_g[0]),
        "w_o": w_o[0].astype(_BF16), "ln1_g": row(ln1_g[0]), "ln1_b": row(ln1_b[0]),
        "wu": _gate_val_chunks(w_up).astype(_BF16),
        "cw": _gate_val_chunks(fcw).astype(_F32),
        "cb": _gate_val_chunks(fcb).astype(_F32),
        "wd": w_down[0].reshape(N_FF_CHUNKS, FF_CHUNK, D_MODEL).astype(_BF16),
        "ln2_g": row(ln2_g[0]), "ln2_b": row(ln2_b[0]),
    }


def kernel(x_prompt, x_sample, ln_in_g, ln_in_b, w_in, conv_w, conv_b, rpb, gn_g, w_o, ln1_g,
           ln1_b, w_up, ffn_conv_w, ffn_conv_b, w_down, ln2_g, ln2_b):
    p = _prepare(ln_in_g, ln_in_b, w_in, conv_w, conv_b, rpb, gn_g, w_o, ln1_g, ln1_b,
                 w_up, ffn_conv_w, ffn_conv_b, w_down, ln2_g, ln2_b)
    return (_trunk(x_prompt, p), _trunk(x_sample, p))
```
